```python
import math
import jax, jax.numpy as jnp
from jax import lax
import numpy as np

D_MODEL = 4096
BATCH = 1
SEQ = 8192
DEPTH = 2

CHUNK = 64
D_MIX = D_MODEL
GDN_HEAD_DIM = 128
GDN_WIDTH = (3 * D_MIX) // 4
GDN_HEADS = GDN_WIDTH // GDN_HEAD_DIM
CONV_WIDTH = 4
DT_MIN = 1e-3
DT_MAX = 1e-1
S5_WIDTH = D_MIX - GDN_WIDTH
S5_GROUP = 16
S5_GROUPS = S5_WIDTH // S5_GROUP
S5_STATE = 64
PROJ_COLS = 4 * GDN_WIDTH + 2 * GDN_HEADS + S5_WIDTH
N_GROUPS = 4
EXPERTS_PER_GROUP = 8
N_EXPERTS = N_GROUPS * EXPERTS_PER_GROUP
TOP_K = 2
D_EXPERT = D_MODEL // 8
MOE_BLOCK = 128
DN_ALPHA = (2 * DEPTH) ** 0.25
DN_BETA = (8 * DEPTH) ** -0.25
LN_EPS = 1e-5
RMS_EPS = 1e-6
L2_EPS = 1e-6

kernel_name = "hybrid_gdn_s5_hmoe_deepnorm"


def layer_norm(x, g, b):
    xf = x.astype(jnp.float32)
    mu = jnp.mean(xf, axis=-1, keepdims=True)
    var = jnp.mean(jnp.square(xf - mu), axis=-1, keepdims=True)
    return ((xf - mu) * lax.rsqrt(var + LN_EPS) * g + b).astype(x.dtype)


def causal_depthwise_conv(x, w):
    k = w.shape[0]
    xp = jnp.pad(x, ((0, 0), (k - 1, 0), (0, 0)))
    return lax.conv_general_dilated(xp, w[:, None, :], window_strides=(1,), padding='VALID',
                                    dimension_numbers=('NWC', 'WIO', 'NWC'),
                                    feature_group_count=x.shape[-1])


def chunk_gated_delta_rule(q, k, v, g, beta):
    bsz, l, h, dk = q.shape
    dv = v.shape[-1]
    n = l // CHUNK

    def to_chunks(t):
        t = t.reshape((bsz, n, CHUNK, h) + t.shape[3:])
        return jnp.moveaxis(t, 3, 1)

    q, k, v, g, beta = (to_chunks(t) for t in (q, k, v, g, beta))
    q = q * (dk ** -0.5)
    decay = jnp.cumsum(g, axis=-1)
    causal = jnp.tril(jnp.ones((CHUNK, CHUNK), dtype=bool))
    strict = jnp.tril(jnp.ones((CHUNK, CHUNK), dtype=bool), k=-1)
    diff = decay[..., :, None] - decay[..., None, :]
    gamma = jnp.where(causal, jnp.exp(jnp.where(causal, diff, 0.0)), 0.0)
    k_beta = k * beta[..., None]
    kk = jnp.einsum('bhnid,bhnjd->bhnij', k_beta, k) * gamma
    t_mat = jnp.eye(CHUNK, dtype=q.dtype) + jnp.where(strict, kk, 0.0)
    u = lax.linalg.triangular_solve(t_mat, v * beta[..., None], left_side=True, lower=True,
                                    unit_diagonal=True)
    w = lax.linalg.triangular_solve(t_mat, k_beta * jnp.exp(decay)[..., None], left_side=True,
                                    lower=True, unit_diagonal=True)
    qk = jnp.einsum('bhnid,bhnjd->bhnij', q, k) * gamma
    decay_last = decay[..., -1]
    k_tail = k * jnp.exp(decay_last[..., None] - decay)[..., None]
    q_dec = q * jnp.exp(decay)[..., None]

    def step(s, inp):
        qk_c, u_c, w_c, qd_c, kt_c, dl_c = inp
        v_new = u_c - jnp.einsum('bhcd,bhde->bhce', w_c, s)
        o = jnp.einsum('bhcd,bhde->bhce', qd_c, s) + jnp.einsum('bhij,bhje->bhie', qk_c, v_new)
        s = s * jnp.exp(dl_c)[..., None, None] + jnp.einsum('bhcd,bhce->bhde', kt_c, v_new)
        return s, o

    xs = tuple(jnp.moveaxis(t, 2, 0) for t in (qk, u, w, q_dec, k_tail, decay_last))
    s0 = jnp.zeros((bsz, h, dk, dv), q.dtype)
    _, o = lax.scan(step, s0, xs)
    return jnp.transpose(o, (1, 0, 3, 2, 4)).reshape(bsz, l, h, dv)


def gated_deltanet(qkv, z, a, b, conv_w, a_log, dt_bias, norm_w):
    bsz, l = qkv.shape[:2]
    out_dtype = qkv.dtype
    f32 = jnp.float32
    qkv = jax.nn.silu(causal_depthwise_conv(qkv, conv_w)).astype(f32)
    q, k, v = jnp.split(qkv, 3, axis=-1)

    def heads(t):
        return t.reshape(bsz, l, GDN_HEADS, GDN_HEAD_DIM)

    q, k, v = heads(q), heads(k), heads(v)
    q = q * lax.rsqrt(jnp.sum(q * q, axis=-1, keepdims=True) + L2_EPS)
    k = k * lax.rsqrt(jnp.sum(k * k, axis=-1, keepdims=True) + L2_EPS)
    g = -jnp.exp(a_log.astype(f32)) * jax.nn.softplus(a.astype(f32) + dt_bias.astype(f32))
    beta = jax.nn.sigmoid(b.astype(f32))
    o = chunk_gated_delta_rule(q, k, v, g, beta)
    o = o * lax.rsqrt(jnp.mean(o * o, axis=-1, keepdims=True) + RMS_EPS) * norm_w.astype(f32)
    o = o * jax.nn.silu(heads(z.astype(f32)))
    return o.reshape(bsz, l, GDN_WIDTH).astype(out_dtype)


def s5_layer(u, lambda_re, lambda_im, b_re, b_im, c_re, c_im, d_skip, log_dt, w_glu):
    bsz, l = u.shape[:2]
    out_dtype = u.dtype
    f32 = jnp.float32
    uf = u.astype(f32).reshape(bsz, l, S5_GROUPS, S5_GROUP)
    lr, li = lambda_re.astype(f32), lambda_im.astype(f32)
    dt = jnp.exp(log_dt.astype(f32))[:, None]
    mag = jnp.exp(lr * dt)
    ab_re, ab_im = mag * jnp.cos(li * dt), mag * jnp.sin(li * dt)
    den = lr * lr + li * li
    nr, ni = ab_re - 1.0, ab_im
    coef_re = (nr * lr + ni * li) / den
    coef_im = (ni * lr - nr * li) / den
    br, bi = b_re.astype(f32), b_im.astype(f32)
    bb_re = coef_re[..., None] * br - coef_im[..., None] * bi
    bb_im = coef_re[..., None] * bi + coef_im[..., None] * br
    bu_re = jnp.einsum('blgh,gph->blgp', uf, bb_re)
    bu_im = jnp.einsum('blgh,gph->blgp', uf, bb_im)
    a_re = jnp.broadcast_to(ab_re, bu_re.shape)
    a_im = jnp.broadcast_to(ab_im, bu_re.shape)

    def combine(e1, e2):
        a1r, a1i, b1r, b1i = e1
        a2r, a2i, b2r, b2i = e2
        return (a2r * a1r - a2i * a1i, a2r * a1i + a2i * a1r,
                a2r * b1r - a2i * b1i + b2r, a2r * b1i + a2i * b1r + b2i)

    _, _, x_re, x_im = lax.associative_scan(combine, (a_re, a_im, bu_re, bu_im), axis=1)
    y = (jnp.einsum('blgp,ghp->blgh', x_re, c_re.astype(f32))
         - jnp.einsum('blgp,ghp->blgh', x_im, c_im.astype(f32)))
    y = y.reshape(bsz, l, S5_WIDTH) + d_skip.astype(f32) * uf.reshape(bsz, l, S5_WIDTH)
    y = jax.nn.gelu(y)
    y = y * jax.nn.sigmoid(y @ w_glu.astype(f32))
    return y.astype(out_dtype)


def hierarchical_moe(x, w_rg, b_rg, w_re, b_re, w_gate, w_up, w_down):
    bsz, l, d = x.shape
    f32 = jnp.float32
    n_tok = bsz * l
    xt = x.reshape(n_tok, d)
    grp_logits = (xt @ w_rg).astype(f32) + b_rg.astype(f32)
    grp_p, grp_idx = lax.top_k(jax.nn.softmax(grp_logits, axis=-1), 1)
    exp_logits = jnp.einsum('td,gde->tge', xt, w_re).astype(f32) + b_re.astype(f32)
    sel = exp_logits[jnp.arange(n_tok), grp_idx[:, 0]]
    top_logit, top_idx = lax.top_k(sel, TOP_K)
    top_w = jax.nn.softmax(top_logit, axis=-1) * grp_p
    expert_id = (grp_idx * EXPERTS_PER_GROUP + top_idx).reshape(-1)
    n_assign = n_tok * TOP_K
    tok_id = jnp.arange(n_assign, dtype=jnp.int32) // TOP_K
    wts = top_w.reshape(-1)
    order = jnp.argsort(expert_id)
    e_sorted = expert_id[order]
    counts = jnp.zeros((N_EXPERTS,), jnp.int32).at[expert_id].add(1)
    start = jnp.cumsum(counts) - counts
    padded = (counts + MOE_BLOCK - 1) // MOE_BLOCK * MOE_BLOCK
    pend = jnp.cumsum(padded)
    pstart = pend - padded
    dest = pstart[e_sorted] + (jnp.arange(n_assign, dtype=jnp.int32) - start[e_sorted])
    n_blocks = (n_assign + MOE_BLOCK - 1) // MOE_BLOCK + N_EXPERTS
    n_rows = n_blocks * MOE_BLOCK
    row_tok = jnp.zeros((n_rows,), jnp.int32).at[dest].set(tok_id[order])
    row_w = jnp.zeros((n_rows,), f32).at[dest].set(wts[order])
    blk_start = jnp.arange(n_blocks, dtype=jnp.int32) * MOE_BLOCK
    blk_expert = jnp.minimum(jnp.searchsorted(pend, blk_start, side='right'), N_EXPERTS - 1)
    xb = xt[row_tok].reshape(n_blocks, MOE_BLOCK, d)

    def expert_block(args):
        xblk, e = args
        hid = jax.nn.silu(xblk @ w_gate[e]) * (xblk @ w_up[e])
        return hid @ w_down[e]

    yb = lax.map(expert_block, (xb, blk_expert)).reshape(n_rows, d)
    y = jax.ops.segment_sum(yb * row_w[:, None].astype(yb.dtype), row_tok, num_segments=n_tok)
    return y.reshape(bsz, l, d)


def setup_inputs(seed: int = 0) -> dict:
    key = jax.random.key(seed)
    ks = jax.random.split(key, 32)
    f32 = jnp.float32

    def nrm(k, shape, scale):
        return jax.random.normal(k, shape, f32) * scale

    x = nrm(ks[0], (BATCH, SEQ, D_MODEL), 1.0)
    w_in = nrm(ks[1], (DEPTH, D_MODEL, PROJ_COLS), D_MODEL ** -0.5)
    gdn_conv_w = nrm(ks[2], (DEPTH, CONV_WIDTH, 3 * GDN_WIDTH), CONV_WIDTH ** -0.5)
    gdn_a_log = jnp.log(jax.random.uniform(ks[3], (DEPTH, GDN_HEADS), f32, 1.0, 16.0))
    dt = jnp.exp(jax.random.uniform(ks[4], (DEPTH, GDN_HEADS), f32, math.log(DT_MIN), math.log(DT_MAX)))
    gdn_dt_bias = dt + jnp.log(-jnp.expm1(-dt))
    gdn_norm_w = 1.0 + nrm(ks[5], (DEPTH, GDN_HEAD_DIM), 0.01)
    n_idx = jnp.arange(S5_STATE, dtype=f32)
    s5_lambda_re = -0.5 + nrm(ks[6], (DEPTH, S5_GROUPS, S5_STATE), 0.01)
    s5_lambda_im = math.pi * n_idx + nrm(ks[7], (DEPTH, S5_GROUPS, S5_STATE), 0.01)
    b_scale = (2 * S5_GROUP) ** -0.5
    s5_b_re = nrm(ks[8], (DEPTH, S5_GROUPS, S5_STATE, S5_GROUP), b_scale)
    s5_b_im = nrm(ks[9], (DEPTH, S5_GROUPS, S5_STATE, S5_GROUP), b_scale)
    c_scale = (2 * S5_STATE) ** -0.5
    s5_c_re = nrm(ks[10], (DEPTH, S5_GROUPS, S5_GROUP, S5_STATE), c_scale)
    s5_c_im = nrm(ks[11], (DEPTH, S5_GROUPS, S5_GROUP, S5_STATE), c_scale)
    s5_d = nrm(ks[12], (DEPTH, S5_WIDTH), 1.0)
    s5_log_dt = jax.random.uniform(ks[13], (DEPTH, S5_GROUPS), f32, math.log(DT_MIN), math.log(DT_MAX))
    s5_w_glu = nrm(ks[14], (DEPTH, S5_WIDTH, S5_WIDTH), S5_WIDTH ** -0.5)
    w_out = nrm(ks[15], (DEPTH, D_MIX, D_MODEL), DN_BETA * D_MIX ** -0.5)
    ln1_g = 1.0 + nrm(ks[16], (DEPTH, D_MODEL), 0.01)
    ln1_b = nrm(ks[17], (DEPTH, D_MODEL), 0.01)
    router_group_w = nrm(ks[18], (DEPTH, D_MODEL, N_GROUPS), D_MODEL ** -0.5)
    router_group_b = nrm(ks[19], (DEPTH, N_GROUPS), 0.01)
    router_expert_w = nrm(ks[20], (DEPTH, N_GROUPS, D_MODEL, EXPERTS_PER_GROUP), D_MODEL ** -0.5)
    router_expert_b = nrm(ks[21], (DEPTH, N_GROUPS, EXPERTS_PER_GROUP), 0.01)
    expert_w_gate = nrm(ks[22], (DEPTH, N_EXPERTS, D_MODEL, D_EXPERT), D_MODEL ** -0.5)
    expert_w_up = nrm(ks[23], (DEPTH, N_EXPERTS, D_MODEL, D_EXPERT), D_MODEL ** -0.5)
    expert_w_down = nrm(ks[24], (DEPTH, N_EXPERTS, D_EXPERT, D_MODEL), DN_BETA * D_EXPERT ** -0.5)
    ln2_g = 1.0 + nrm(ks[25], (DEPTH, D_MODEL), 0.01)
    ln2_b = nrm(ks[26], (DEPTH, D_MODEL), 0.01)
    return {"x": x, "w_in": w_in, "gdn_conv_w": gdn_conv_w, "gdn_a_log": gdn_a_log,
            "gdn_dt_bias": gdn_dt_bias, "gdn_norm_w": gdn_norm_w,
            "s5_lambda_re": s5_lambda_re, "s5_lambda_im": s5_lambda_im,
            "s5_b_re": s5_b_re, "s5_b_im": s5_b_im, "s5_c_re": s5_c_re, "s5_c_im": s5_c_im,
            "s5_d": s5_d, "s5_log_dt": s5_log_dt, "s5_w_glu": s5_w_glu, "w_out": w_out,
            "ln1_g": ln1_g, "ln1_b": ln1_b, "router_group_w": router_group_w,
            "router_group_b": router_group_b, "router_expert_w": router_expert_w,
            "router_expert_b": router_expert_b, "expert_w_gate": expert_w_gate,
            "expert_w_up": expert_w_up, "expert_w_down": expert_w_down,
            "ln2_g": ln2_g, "ln2_b": ln2_b}


def reference(x, w_in, gdn_conv_w, gdn_a_log, gdn_dt_bias, gdn_norm_w,
              s5_lambda_re, s5_lambda_im, s5_b_re, s5_b_im, s5_c_re, s5_c_im,
              s5_d, s5_log_dt, s5_w_glu, w_out, ln1_g, ln1_b,
              router_group_w, router_group_b, router_expert_w, router_expert_b,
              expert_w_gate, expert_w_up, expert_w_down, ln2_g, ln2_b):
    c_q = 3 * GDN_WIDTH
    c_z = 4 * GDN_WIDTH
    c_a = c_z + GDN_HEADS
    c_b = c_a + GDN_HEADS
    for i in range(DEPTH):
        proj = x @ w_in[i]
        y_gdn = gated_deltanet(proj[..., :c_q], proj[..., c_q:c_z], proj[..., c_z:c_a],
                               proj[..., c_a:c_b], gdn_conv_w[i], gdn_a_log[i],
                               gdn_dt_bias[i], gdn_norm_w[i])
        y_s5 = s5_layer(proj[..., c_b:], s5_lambda_re[i], s5_lambda_im[i], s5_b_re[i],
                        s5_b_im[i], s5_c_re[i], s5_c_im[i], s5_d[i], s5_log_dt[i], s5_w_glu[i])
        mix = jnp.concatenate([y_gdn, y_s5], axis=-1) @ w_out[i]
        x = layer_norm(DN_ALPHA * x + mix, ln1_g[i], ln1_b[i])
        ffn = hierarchical_moe(x, router_group_w[i], router_group_b[i], router_expert_w[i],
                               router_expert_b[i], expert_w_gate[i], expert_w_up[i],
                               expert_w_down[i])
        x = layer_norm(DN_ALPHA * x + ffn, ln2_g[i], ln2_b[i])
    return x
```

```python
import functools
import math

import jax
import jax.numpy as jnp
from jax import lax
from jax.experimental import pallas as pl
from jax.experimental.pallas import tpu as pltpu

F32 = jnp.float32
BF16 = jnp.bfloat16

LANES = 128
SUBLANES = 8
VMEM_LIMIT_BYTES = 56 * 1024 * 1024

GDN_CHUNK = 64
GDN_HEAD_DIM = 128
CONV_WIDTH = 4
S5_GROUP = 16
S5_STATE = 64
S5_GROUPS_PER_SLAB = LANES // S5_GROUP
S5_SLAB_STATES = S5_GROUPS_PER_SLAB * S5_STATE
S5_TILES_PER_SLAB = S5_SLAB_STATES // LANES
N_GROUPS = 4
EXPERTS_PER_GROUP = 8
TOP_K = 2
LN_EPS = 1e-5
RMS_EPS = 1e-6
L2_EPS = 1e-6


def _params(sem):
    return pltpu.CompilerParams(dimension_semantics=sem, vmem_limit_bytes=VMEM_LIMIT_BYTES)


def _pick_tile(n, candidates):
    for c in candidates:
        if n % c == 0:
            return c
    return n


def _bdot(a, b):
    return jnp.dot(a.astype(BF16), b.astype(BF16), preferred_element_type=F32)


def _bdot_nt(a, b):
    return lax.dot_general(a.astype(BF16), b.astype(BF16), (((1,), (1,)), ((), ())),
                           preferred_element_type=F32)


def _bdot_tn(a, b):
    return lax.dot_general(a.astype(BF16), b.astype(BF16), (((0,), (0,)), ((), ())),
                           preferred_element_type=F32)


def _sigmoid(x):
    return 1.0 / (1.0 + jnp.exp(-x))


def _mm_kernel(a_ref, b_ref, o_ref, acc_ref):
    k = pl.program_id(2)

    @pl.when(k == 0)
    def _():
        acc_ref[...] = jnp.zeros_like(acc_ref)

    acc_ref[...] += jnp.dot(a_ref[...], b_ref[...], preferred_element_type=F32)

    @pl.when(k == pl.num_programs(2) - 1)
    def _():
        o_ref[...] = acc_ref[...].astype(o_ref.dtype)


def _matmul(a, b, out_dtype):
    m, kdim = a.shape
    n = b.shape[1]
    tm = _pick_tile(m, (1024, 512, 256, 128))
    tn = _pick_tile(n, (1024, 512, 256, 128))
    tk = _pick_tile(kdim, (1024, 512, 256, 128))
    return pl.pallas_call(
        _mm_kernel,
        out_shape=jax.ShapeDtypeStruct((m, n), out_dtype),
        grid=(m // tm, n // tn, kdim // tk),
        in_specs=[pl.BlockSpec((tm, tk), lambda i, j, k: (i, k)),
                  pl.BlockSpec((tk, tn), lambda i, j, k: (k, j))],
        out_specs=pl.BlockSpec((tm, tn), lambda i, j, k: (i, j)),
        scratch_shapes=[pltpu.VMEM((tm, tn), F32)],
        compiler_params=_params(("parallel", "parallel", "arbitrary")),
        name="matmul",
    )(a, b)


def _neumann_inverse(a_strict):
    n = a_strict.shape[0]
    ii = lax.broadcasted_iota(jnp.int32, (n, n), 0)
    jj = lax.broadcasted_iota(jnp.int32, (n, n), 1)
    eye = jnp.where(ii == jj, 1.0, 0.0).astype(F32)
    p = eye - a_strict
    pw = a_strict
    span = 1
    while 2 * span < n:
        pw = _bdot(pw, pw)
        p = p + _bdot(p, pw)
        span *= 2
    return p


def _gdn_kernel(q_ref, k_ref, v_ref, z_ref, wq_ref, wk_ref, wv_ref, a_ref, b_ref,
                alog_ref, dtb_ref, nw_ref, o_ref,
                s_ref, cq_ref, ck_ref, cv_ref, qs_ref, ks_ref, vs_ref, *, tb, hb):
    c = GDN_CHUNK
    d = GDN_HEAD_DIM
    nc = tb // c

    @pl.when(pl.program_id(1) == 0)
    def _():
        s_ref[...] = jnp.zeros_like(s_ref)
        cq_ref[0:SUBLANES, :] = jnp.zeros((SUBLANES, hb * d), F32)
        ck_ref[0:SUBLANES, :] = jnp.zeros((SUBLANES, hb * d), F32)
        cv_ref[0:SUBLANES, :] = jnp.zeros((SUBLANES, hb * d), F32)

    def conv_silu(x_ref, c_ref, w_ref):
        c_ref[SUBLANES:SUBLANES + tb, :] = x_ref[...]
        w = w_ref[...]
        acc = c_ref[SUBLANES:SUBLANES + tb, :] * w[CONV_WIDTH - 1:CONV_WIDTH, :]
        for j in range(1, CONV_WIDTH):
            acc = acc + (c_ref[SUBLANES - j:SUBLANES - j + tb, :]
                         * w[CONV_WIDTH - 1 - j:CONV_WIDTH - j, :])
        c_ref[0:SUBLANES, :] = c_ref[tb:tb + SUBLANES, :]
        return acc * _sigmoid(acc)

    qf = conv_silu(q_ref, cq_ref, wq_ref)
    kf = conv_silu(k_ref, ck_ref, wk_ref)
    vs_ref[...] = conv_silu(v_ref, cv_ref, wv_ref)
    for h in range(hb):
        sl = slice(h * d, (h + 1) * d)
        qh = qf[:, sl]
        kh = kf[:, sl]
        qs_ref[:, sl] = qh * lax.rsqrt(jnp.sum(qh * qh, axis=-1, keepdims=True) + L2_EPS) * (d ** -0.5)
        ks_ref[:, sl] = kh * lax.rsqrt(jnp.sum(kh * kh, axis=-1, keepdims=True) + L2_EPS)

    ii = lax.broadcasted_iota(jnp.int32, (c, c), 0)
    jj = lax.broadcasted_iota(jnp.int32, (c, c), 1)
    causal = jj <= ii
    strict = jj < ii
    diag = jj == ii
    nw = nw_ref[...]

    def chunk_body(ci, carry):
        r0 = pl.multiple_of(ci * c, c)
        for h in range(hb):
            sl = slice(h * d, (h + 1) * d)
            qc = qs_ref[pl.ds(r0, c), sl]
            kc = ks_ref[pl.ds(r0, c), sl]
            vc = vs_ref[pl.ds(r0, c), sl]
            zc = z_ref[pl.ds(r0, c), sl]
            a_row = a_ref[h, pl.ds(ci, 1), :]
            b_row = b_ref[h, pl.ds(ci, 1), :]
            sp_in = a_row + dtb_ref[h]
            softplus = jnp.maximum(sp_in, 0.0) + jnp.log1p(jnp.exp(-jnp.abs(sp_in)))
            g_row = -jnp.exp(alog_ref[h]) * softplus
            beta_row = _sigmoid(b_row)
            g_b = jnp.broadcast_to(g_row, (c, c))
            decay_col = jnp.sum(jnp.where(causal, g_b, 0.0), axis=1, keepdims=True)
            decay_row = jnp.sum(jnp.where(diag, jnp.broadcast_to(decay_col, (c, c)), 0.0),
                                axis=0, keepdims=True)
            beta_col = jnp.sum(jnp.where(diag, jnp.broadcast_to(beta_row, (c, c)), 0.0),
                               axis=1, keepdims=True)
            gamma = jnp.where(causal, jnp.exp(jnp.where(causal, decay_col - decay_row, 0.0)), 0.0)
            k_beta = kc * beta_col
            a_strict = jnp.where(strict, _bdot_nt(k_beta, kc) * gamma, 0.0)
            t_inv = _neumann_inverse(a_strict)
            e_col = jnp.exp(decay_col)
            u = _bdot(t_inv, vc * beta_col)
            w = _bdot(t_inv, k_beta * e_col)
            qk = _bdot_nt(qc, kc) * gamma
            decay_last = decay_col[c - 1:c, :]
            k_tail = kc * jnp.exp(decay_last - decay_col)
            q_dec = qc * e_col
            s = s_ref[h]
            v_new = u - _bdot(w, s)
            o = _bdot(q_dec, s) + _bdot(qk, v_new)
            s_ref[h] = s * jnp.exp(decay_last) + _bdot_tn(k_tail, v_new)
            o = o * lax.rsqrt(jnp.mean(o * o, axis=-1, keepdims=True) + RMS_EPS) * nw
            o = o * (zc * _sigmoid(zc))
            o_ref[pl.ds(r0, c), sl] = o.astype(o_ref.dtype)
        return carry

    lax.fori_loop(0, nc, chunk_body, 0)


def _gdn(proj, conv_w, a_t, b_t, a_log, dt_bias, norm_w, *, heads):
    seq = proj.shape[0]
    d = GDN_HEAD_DIM
    hb = 2 if heads % 2 == 0 else 1
    tb = _pick_tile(seq, (512,))
    n_hb = heads // hb
    wblk = hb * d
    kernel = functools.partial(_gdn_kernel, tb=tb, hb=hb)

    def col(offset):
        return lambda h, t: (t, offset * n_hb + h)

    def wcol(offset):
        return lambda h, t: (0, offset * n_hb + h)

    return pl.pallas_call(
        kernel,
        out_shape=jax.ShapeDtypeStruct((seq, heads * d), BF16),
        grid=(n_hb, seq // tb),
        in_specs=[pl.BlockSpec((tb, wblk), col(0)),
                  pl.BlockSpec((tb, wblk), col(1)),
                  pl.BlockSpec((tb, wblk), col(2)),
                  pl.BlockSpec((tb, wblk), col(3)),
                  pl.BlockSpec((CONV_WIDTH, wblk), wcol(0)),
                  pl.BlockSpec((CONV_WIDTH, wblk), wcol(1)),
                  pl.BlockSpec((CONV_WIDTH, wblk), wcol(2)),
                  pl.BlockSpec((hb, tb // GDN_CHUNK, GDN_CHUNK), lambda h, t: (h, t, 0)),
                  pl.BlockSpec((hb, tb // GDN_CHUNK, GDN_CHUNK), lambda h, t: (h, t, 0)),
                  pl.BlockSpec((hb, 1, 1), lambda h, t: (h, 0, 0)),
                  pl.BlockSpec((hb, 1, 1), lambda h, t: (h, 0, 0)),
                  pl.BlockSpec((1, d), lambda h, t: (0, 0))],
        out_specs=pl.BlockSpec((tb, wblk), lambda h, t: (t, h)),
        scratch_shapes=[pltpu.VMEM((hb, d, d), F32),
                        pltpu.VMEM((tb + SUBLANES, wblk), F32),
                        pltpu.VMEM((tb + SUBLANES, wblk), F32),
                        pltpu.VMEM((tb + SUBLANES, wblk), F32),
                        pltpu.VMEM((tb, wblk), F32),
                        pltpu.VMEM((tb, wblk), F32),
                        pltpu.VMEM((tb, wblk), F32)],
        compiler_params=_params(("parallel", "arbitrary")),
        name="gdn",
    )(proj, proj, proj, proj, conv_w, conv_w, conv_w, a_t, b_t,
      a_log.reshape(heads, 1, 1), dt_bias.reshape(heads, 1, 1), norm_w.reshape(1, d))


def _gelu_tanh(y):
    return 0.5 * y * (1.0 + jnp.tanh(math.sqrt(2.0 / math.pi) * (y + 0.044715 * (y * y * y))))


def _s5_kernel(u_ref, bre_ref, bim_ref, cre_ref, cim_ref, are_ref, aim_ref, d_ref, wglu_ref,
               o_ref, xr_ref, xi_ref, str_ref, sti_ref, y_ref, *, tb, pitch, slabs):
    tiles = slabs * S5_TILES_PER_SLAB
    n_vreg = tiles // SUBLANES

    @pl.when(pl.program_id(0) == 0)
    def _():
        str_ref[...] = jnp.zeros_like(str_ref)
        sti_ref[...] = jnp.zeros_like(sti_ref)

    for s in range(slabs):
        ub = u_ref[:, s * LANES:(s + 1) * LANES].astype(BF16)
        r = jnp.dot(ub, bre_ref[s], preferred_element_type=F32)
        m = jnp.dot(ub, bim_ref[s], preferred_element_type=F32)
        for t4 in range(S5_TILES_PER_SLAB):
            lt = s * S5_TILES_PER_SLAB + t4
            xr_ref[lt * pitch:lt * pitch + tb, :] = r[:, t4 * LANES:(t4 + 1) * LANES]
            xi_ref[lt * pitch:lt * pitch + tb, :] = m[:, t4 * LANES:(t4 + 1) * LANES]

    ar = [are_ref[j] for j in range(n_vreg)]
    ai = [aim_ref[j] for j in range(n_vreg)]

    def step(t, carry):
        xr, xi = carry
        nxr, nxi = [], []
        for j in range(n_vreg):
            rows = pl.ds(j * SUBLANES * pitch + t, SUBLANES, stride=pitch)
            br = xr_ref[rows, :]
            bi = xi_ref[rows, :]
            r = ar[j] * xr[j] - ai[j] * xi[j] + br
            m = ar[j] * xi[j] + ai[j] * xr[j] + bi
            xr_ref[rows, :] = r
            xi_ref[rows, :] = m
            nxr.append(r)
            nxi.append(m)
        return tuple(nxr), tuple(nxi)

    x0 = (tuple(str_ref[j] for j in range(n_vreg)), tuple(sti_ref[j] for j in range(n_vreg)))
    xr_f, xi_f = lax.fori_loop(0, tb, step, x0)
    for j in range(n_vreg):
        str_ref[j] = xr_f[j]
        sti_ref[j] = xi_f[j]

    for s in range(slabs):
        acc = jnp.zeros((tb, LANES), F32)
        for t4 in range(S5_TILES_PER_SLAB):
            lt = s * S5_TILES_PER_SLAB + t4
            xr = xr_ref[lt * pitch:lt * pitch + tb, :].astype(BF16)
            xi = xi_ref[lt * pitch:lt * pitch + tb, :].astype(BF16)
            acc = acc + jnp.dot(xr, cre_ref[s, t4 * LANES:(t4 + 1) * LANES, :],
                                preferred_element_type=F32)
            acc = acc - jnp.dot(xi, cim_ref[s, t4 * LANES:(t4 + 1) * LANES, :],
                                preferred_element_type=F32)
        sl = slice(s * LANES, (s + 1) * LANES)
        y_ref[:, sl] = _gelu_tanh(acc + d_ref[:, sl] * u_ref[:, sl])

    y = y_ref[...]
    gate = jnp.dot(y.astype(BF16), wglu_ref[...], preferred_element_type=F32)
    o_ref[...] = (y * _sigmoid(gate)).astype(o_ref.dtype)


def _s5(proj, u_col_block, lam_re, lam_im, b_re, b_im, c_re, c_im, d_skip, log_dt, w_glu):
    seq = proj.shape[0]
    groups = lam_re.shape[0]
    width = groups * S5_GROUP
    slabs = width // LANES
    tiles = slabs * S5_TILES_PER_SLAB
    assert tiles % SUBLANES == 0
    tb = _pick_tile(seq, (512,))
    pitch = tb + SUBLANES

    dt = jnp.exp(log_dt)[:, None]
    mag = jnp.exp(lam_re * dt)
    ab_re, ab_im = mag * jnp.cos(lam_im * dt), mag * jnp.sin(lam_im * dt)
    den = lam_re * lam_re + lam_im * lam_im
    nr, ni = ab_re - 1.0, ab_im
    coef_re = (nr * lam_re + ni * lam_im) / den
    coef_im = (ni * lam_re - nr * lam_im) / den
    bb_re = coef_re[..., None] * b_re - coef_im[..., None] * b_im
    bb_im = coef_re[..., None] * b_im + coef_im[..., None] * b_re
    eye = jnp.eye(S5_GROUPS_PER_SLAB, dtype=F32)

    def b_blockdiag(bb):
        t = bb.reshape(slabs, S5_GROUPS_PER_SLAB, S5_STATE, S5_GROUP)
        m = jnp.einsum('saph,ab->sahbp', t, eye)
        return m.reshape(slabs, LANES, S5_SLAB_STATES).astype(BF16)

    def c_blockdiag(cc):
        t = cc.reshape(slabs, S5_GROUPS_PER_SLAB, S5_GROUP, S5_STATE)
        m = jnp.einsum('sahp,ab->sapbh', t, eye)
        return m.reshape(slabs, S5_SLAB_STATES, LANES).astype(BF16)

    n_vreg = tiles // SUBLANES
    kernel = functools.partial(_s5_kernel, tb=tb, pitch=pitch, slabs=slabs)
    full3 = lambda i: (0, 0, 0)
    return pl.pallas_call(
        kernel,
        out_shape=jax.ShapeDtypeStruct((seq, width), BF16),
        grid=(seq // tb,),
        in_specs=[pl.BlockSpec((tb, width), lambda i: (i, u_col_block)),
                  pl.BlockSpec((slabs, LANES, S5_SLAB_STATES), full3),
                  pl.BlockSpec((slabs, LANES, S5_SLAB_STATES), full3),
                  pl.BlockSpec((slabs, S5_SLAB_STATES, LANES), full3),
                  pl.BlockSpec((slabs, S5_SLAB_STATES, LANES), full3),
                  pl.BlockSpec((n_vreg, SUBLANES, LANES), full3),
                  pl.BlockSpec((n_vreg, SUBLANES, LANES), full3),
                  pl.BlockSpec((1, width), lambda i: (0, 0)),
                  pl.BlockSpec((width, width), lambda i: (0, 0))],
        out_specs=pl.BlockSpec((tb, width), lambda i: (i, 0)),
        scratch_shapes=[pltpu.VMEM((tiles * pitch, LANES), F32),
                        pltpu.VMEM((tiles * pitch, LANES), F32),
                        pltpu.VMEM((n_vreg, SUBLANES, LANES), F32),
                        pltpu.VMEM((n_vreg, SUBLANES, LANES), F32),
                        pltpu.VMEM((tb, width), F32)],
        compiler_params=_params(("arbitrary",)),
        name="s5",
    )(proj, b_blockdiag(bb_re), b_blockdiag(bb_im), c_blockdiag(c_re), c_blockdiag(c_im),
      ab_re.reshape(n_vreg, SUBLANES, LANES), ab_im.reshape(n_vreg, SUBLANES, LANES),
      d_skip.reshape(1, width), w_glu.astype(BF16))


def _ln_rows(x_ref, m_ref, g_ref, b_ref, alpha):
    h = alpha * x_ref[...] + m_ref[...]
    mu = jnp.mean(h, axis=-1, keepdims=True)
    hc = h - mu
    var = jnp.mean(hc * hc, axis=-1, keepdims=True)
    return hc * lax.rsqrt(var + LN_EPS) * g_ref[...] + b_ref[...]


def _ln_kernel(x_ref, m_ref, g_ref, b_ref, o_ref, ob_ref, *, alpha):
    y = _ln_rows(x_ref, m_ref, g_ref, b_ref, alpha)
    o_ref[...] = y
    ob_ref[...] = y.astype(BF16)


def _ln_router_kernel(x_ref, m_ref, g_ref, b_ref, wr_ref, o_ref, ob_ref, lg_ref, *, alpha):
    y = _ln_rows(x_ref, m_ref, g_ref, b_ref, alpha)
    o_ref[...] = y
    ob_ref[...] = y.astype(BF16)
    lg_ref[...] = jnp.dot(y, wr_ref[...], preferred_element_type=F32,
                          precision=lax.Precision.HIGHEST)


def _residual_layer_norm(x, mix, g, b, alpha, w_router=None):
    seq, dm = x.shape
    tm = _pick_tile(seq, (256, 128))
    row = pl.BlockSpec((tm, dm), lambda i: (i, 0))
    vec = pl.BlockSpec((1, dm), lambda i: (0, 0))
    out_shape = [jax.ShapeDtypeStruct((seq, dm), F32), jax.ShapeDtypeStruct((seq, dm), BF16)]
    out_specs = [row, row]
    args = [x, mix, g.reshape(1, dm), b.reshape(1, dm)]
    in_specs = [row, row, vec, vec]
    if w_router is None:
        kernel = functools.partial(_ln_kernel, alpha=alpha)
    else:
        kernel = functools.partial(_ln_router_kernel, alpha=alpha)
        nr = w_router.shape[1]
        args.append(w_router)
        in_specs.append(pl.BlockSpec((dm, nr), lambda i: (0, 0)))
        out_shape.append(jax.ShapeDtypeStruct((seq, nr), F32))
        out_specs.append(pl.BlockSpec((tm, nr), lambda i: (i, 0)))
    return pl.pallas_call(
        kernel,
        out_shape=out_shape,
        grid=(seq // tm,),
        in_specs=in_specs,
        out_specs=out_specs,
        compiler_params=_params(("parallel",)),
        name="residual_layer_norm",
    )(*args)


def _moe_kernel(be_ref, x_ref, rw_ref, wg_ref, wu_ref, wd_ref, o_ref):
    del be_ref
    x = x_ref[...]
    g = jnp.dot(x, wg_ref[0], preferred_element_type=F32)
    u = jnp.dot(x, wu_ref[0], preferred_element_type=F32)
    hid = (g * _sigmoid(g)) * u
    y = jnp.dot(hid.astype(BF16), wd_ref[0], preferred_element_type=F32)
    o_ref[...] = y * rw_ref[...]


def _moe_experts(xb, row_w, blk_expert, w_gate, w_up, w_down, *, bm):
    n_rows, dm = xb.shape
    de = w_gate.shape[2]
    n_blocks = n_rows // bm
    grid_spec = pltpu.PrefetchScalarGridSpec(
        num_scalar_prefetch=1,
        grid=(n_blocks,),
        in_specs=[pl.BlockSpec((bm, dm), lambda i, be: (i, 0)),
                  pl.BlockSpec((bm, 1), lambda i, be: (i, 0)),
                  pl.BlockSpec((1, dm, de), lambda i, be: (be[i], 0, 0)),
                  pl.BlockSpec((1, dm, de), lambda i, be: (be[i], 0, 0)),
                  pl.BlockSpec((1, de, dm), lambda i, be: (be[i], 0, 0))],
        out_specs=pl.BlockSpec((bm, dm), lambda i, be: (i, 0)),
    )
    return pl.pallas_call(
        _moe_kernel,
        out_shape=jax.ShapeDtypeStruct((n_rows, dm), F32),
        grid_spec=grid_spec,
        compiler_params=_params(("arbitrary",)),
        name="moe_experts",
    )(blk_expert, xb, row_w, w_gate, w_up, w_down)


def _moe(x_bf, logits, b_rg, b_re, w_gate, w_up, w_down, *, bm):
    n_tok, dm = x_bf.shape
    n_experts = N_GROUPS * EXPERTS_PER_GROUP
    grp_logits = logits[:, :N_GROUPS] + b_rg
    exp_logits = logits[:, N_GROUPS:N_GROUPS + n_experts].reshape(n_tok, N_GROUPS, EXPERTS_PER_GROUP) + b_re
    grp_p, grp_idx = lax.top_k(jax.nn.softmax(grp_logits, axis=-1), 1)
    sel = exp_logits[jnp.arange(n_tok), grp_idx[:, 0]]
    top_logit, top_idx = lax.top_k(sel, TOP_K)
    top_w = jax.nn.softmax(top_logit, axis=-1) * grp_p
    expert_id = (grp_idx * EXPERTS_PER_GROUP + top_idx).reshape(-1).astype(jnp.int32)
    n_assign = n_tok * TOP_K
    tok_id = jnp.arange(n_assign, dtype=jnp.int32) // TOP_K
    wts = top_w.reshape(-1)
    order = jnp.argsort(expert_id)
    e_sorted = expert_id[order]
    counts = jnp.zeros((n_experts,), jnp.int32).at[expert_id].add(1)
    start = jnp.cumsum(counts) - counts
    padded = (counts + bm - 1) // bm * bm
    pend = jnp.cumsum(padded)
    pstart = pend - padded
    dest = pstart[e_sorted] + (jnp.arange(n_assign, dtype=jnp.int32) - start[e_sorted])
    n_blocks = (n_assign + bm - 1) // bm + n_experts
    n_rows = n_blocks * bm
    row_tok = jnp.zeros((n_rows,), jnp.int32).at[dest].set(tok_id[order])
    row_w = jnp.zeros((n_rows,), F32).at[dest].set(wts[order])
    blk_start = jnp.arange(n_blocks, dtype=jnp.int32) * bm
    blk_expert = jnp.minimum(jnp.searchsorted(pend, blk_start, side='right'),
                             n_experts - 1).astype(jnp.int32)
    xb = x_bf[row_tok]
    yb = _moe_experts(xb, row_w[:, None], blk_expert, w_gate, w_up, w_down, bm=bm)
    pos = jnp.zeros((n_assign,), jnp.int32).at[order].set(dest).reshape(n_tok, TOP_K)
    return yb[pos[:, 0]] + yb[pos[:, 1]]


def kernel(x, w_in, gdn_conv_w, gdn_a_log, gdn_dt_bias, gdn_norm_w, s5_lambda_re, s5_lambda_im, s5_b_re, s5_b_im, s5_c_re, s5_c_im, s5_d, s5_log_dt, s5_w_glu, w_out, ln1_g, ln1_b, router_group_w, router_group_b, router_expert_w, router_expert_b, expert_w_gate, expert_w_up, expert_w_down, ln2_g, ln2_b):
    bsz, seq, dm = x.shape
    assert bsz == 1
    depth = w_in.shape[0]
    heads = gdn_a_log.shape[1]
    gdn_width = heads * GDN_HEAD_DIM
    s5_width = s5_d.shape[1]
    c_z = 4 * gdn_width
    c_b = c_z + 2 * heads
    assert (4 * gdn_width) % s5_width == 0
    alpha = (2 * depth) ** 0.25
    n_experts = N_GROUPS * EXPERTS_PER_GROUP
    router_cols = LANES
    moe_bm = 256

    xf = x.reshape(seq, dm)
    xb = xf.astype(BF16)
    for i in range(depth):
        w_main = jnp.concatenate([w_in[i][:, :c_z], w_in[i][:, c_b:]], axis=1).astype(BF16)
        w_ab = jnp.pad(w_in[i][:, c_z:c_b], ((0, 0), (0, LANES - 2 * heads))).astype(BF16)
        proj = _matmul(xb, w_main, F32)
        proj_ab = _matmul(xb, w_ab, F32)
        a_t = proj_ab[:, :heads].T.reshape(heads, seq // GDN_CHUNK, GDN_CHUNK)
        b_t = proj_ab[:, heads:2 * heads].T.reshape(heads, seq // GDN_CHUNK, GDN_CHUNK)
        y_gdn = _gdn(proj, gdn_conv_w[i], a_t, b_t, gdn_a_log[i], gdn_dt_bias[i], gdn_norm_w[i],
                     heads=heads)
        y_s5 = _s5(proj, c_z // s5_width, s5_lambda_re[i], s5_lambda_im[i], s5_b_re[i], s5_b_im[i],
                   s5_c_re[i], s5_c_im[i], s5_d[i], s5_log_dt[i], s5_w_glu[i])
        mix = _matmul(jnp.concatenate([y_gdn, y_s5], axis=1), w_out[i].astype(BF16), F32)
        w_router = jnp.concatenate(
            [router_group_w[i],
             jnp.transpose(router_expert_w[i], (1, 0, 2)).reshape(dm, n_experts),
             jnp.zeros((dm, router_cols - N_GROUPS - n_experts), F32)], axis=1)
        xf, xb, logits = _residual_layer_norm(xf, mix, ln1_g[i], ln1_b[i], alpha, w_router)
        ffn = _moe(xb, logits, router_group_b[i], router_expert_b[i],
                   expert_w_gate[i].astype(BF16), expert_w_up[i].astype(BF16),
                   expert_w_down[i].astype(BF16), bm=moe_bm)
        xf, xb = _residual_layer_norm(xf, ffn, ln2_g[i], ln2_b[i], alpha)
    return xf.reshape(bsz, seq, dm)
```

```python
import functools
import math

import jax
import jax.numpy as jnp
from jax import lax
from jax.experimental import pallas as pl
from jax.experimental.pallas import tpu as pltpu

F32 = jnp.float32
BF16 = jnp.bfloat16

LANES = 128
SUBLANES = 8
VMEM_LIMIT_BYTES = 56 * 1024 * 1024

GDN_CHUNK = 64
GDN_HEAD_DIM = 128
CONV_WIDTH = 4
S5_GROUP = 16
S5_STATE = 64
S5_GROUPS_PER_SLAB = LANES // S5_GROUP
S5_SLAB_STATES = S5_GROUPS_PER_SLAB * S5_STATE
S5_TILES_PER_SLAB = S5_SLAB_STATES // LANES
N_GROUPS = 4
EXPERTS_PER_GROUP = 8
TOP_K = 2
LN_EPS = 1e-5
RMS_EPS = 1e-6
L2_EPS = 1e-6


def _params(sem):
    return pltpu.CompilerParams(dimension_semantics=sem, vmem_limit_bytes=VMEM_LIMIT_BYTES)


def _pick_tile(n, candidates):
    for c in candidates:
        if n % c == 0:
            return c
    return n


def _bdot(a, b):
    return jnp.dot(a.astype(BF16), b.astype(BF16), preferred_element_type=F32)


def _bdot_nt(a, b):
    return lax.dot_general(a.astype(BF16), b.astype(BF16), (((1,), (1,)), ((), ())),
                           preferred_element_type=F32)


def _bdot_tn(a, b):
    return lax.dot_general(a.astype(BF16), b.astype(BF16), (((0,), (0,)), ((), ())),
                           preferred_element_type=F32)


def _sigmoid(x):
    return 1.0 / (1.0 + jnp.exp(-x))


def _mm_kernel(a_ref, b_ref, o_ref, acc_ref):
    k = pl.program_id(2)

    @pl.when(k == 0)
    def _():
        acc_ref[...] = jnp.zeros_like(acc_ref)

    acc_ref[...] += jnp.dot(a_ref[...], b_ref[...], preferred_element_type=F32)

    @pl.when(k == pl.num_programs(2) - 1)
    def _():
        o_ref[...] = acc_ref[...].astype(o_ref.dtype)


def _matmul(a, b, out_dtype):
    m, kdim = a.shape
    n = b.shape[1]
    tm = _pick_tile(m, (1024, 512, 256, 128))
    tn = _pick_tile(n, (1024, 512, 256, 128))
    tk = _pick_tile(kdim, (1024, 512, 256, 128))
    return pl.pallas_call(
        _mm_kernel,
        out_shape=jax.ShapeDtypeStruct((m, n), out_dtype),
        grid=(m // tm, n // tn, kdim // tk),
        in_specs=[pl.BlockSpec((tm, tk), lambda i, j, k: (i, k)),
                  pl.BlockSpec((tk, tn), lambda i, j, k: (k, j))],
        out_specs=pl.BlockSpec((tm, tn), lambda i, j, k: (i, j)),
        scratch_shapes=[pltpu.VMEM((tm, tn), F32)],
        compiler_params=_params(("parallel", "parallel", "arbitrary")),
        name="matmul",
    )(a, b)


def _gdn_kernel(q_ref, k_ref, v_ref, z_ref, wq_ref, wk_ref, wv_ref, a_ref, b_ref,
                alog_ref, dtb_ref, nw_ref, o_ref,
                s_ref, cq_ref, ck_ref, cv_ref, qs_ref, ks_ref, vs_ref, *, tb, hb):
    c = GDN_CHUNK
    d = GDN_HEAD_DIM
    nc = tb // c

    @pl.when(pl.program_id(1) == 0)
    def _():
        s_ref[...] = jnp.zeros_like(s_ref)
        cq_ref[0:SUBLANES, :] = jnp.zeros((SUBLANES, hb * d), F32)
        ck_ref[0:SUBLANES, :] = jnp.zeros((SUBLANES, hb * d), F32)
        cv_ref[0:SUBLANES, :] = jnp.zeros((SUBLANES, hb * d), F32)

    def conv_silu(x_ref, c_ref, w_ref):
        c_ref[SUBLANES:SUBLANES + tb, :] = x_ref[...]
        w = w_ref[...]
        acc = c_ref[SUBLANES:SUBLANES + tb, :] * w[CONV_WIDTH - 1:CONV_WIDTH, :]
        for j in range(1, CONV_WIDTH):
            acc = acc + (c_ref[SUBLANES - j:SUBLANES - j + tb, :]
                         * w[CONV_WIDTH - 1 - j:CONV_WIDTH - j, :])
        c_ref[0:SUBLANES, :] = c_ref[tb:tb + SUBLANES, :]
        return acc * _sigmoid(acc)

    qf = conv_silu(q_ref, cq_ref, wq_ref)
    kf = conv_silu(k_ref, ck_ref, wk_ref)
    vs_ref[...] = conv_silu(v_ref, cv_ref, wv_ref)
    for h in range(hb):
        sl = slice(h * d, (h + 1) * d)
        qh = qf[:, sl]
        kh = kf[:, sl]
        qs_ref[:, sl] = qh * lax.rsqrt(jnp.sum(qh * qh, axis=-1, keepdims=True) + L2_EPS) * (d ** -0.5)
        ks_ref[:, sl] = kh * lax.rsqrt(jnp.sum(kh * kh, axis=-1, keepdims=True) + L2_EPS)

    ii = lax.broadcasted_iota(jnp.int32, (c, c), 0)
    jj = lax.broadcasted_iota(jnp.int32, (c, c), 1)
    causal = jj <= ii
    strict = jj < ii
    diag = jj == ii
    nw = nw_ref[...]

    eye = jnp.where(diag, 1.0, 0.0).astype(F32)
    heads = range(hb)

    def chunk_body(ci, carry):
        r0 = pl.multiple_of(ci * c, c)
        sls = [slice(h * d, (h + 1) * d) for h in heads]
        qc = [qs_ref[pl.ds(r0, c), sl] for sl in sls]
        kc = [ks_ref[pl.ds(r0, c), sl] for sl in sls]
        vc = [vs_ref[pl.ds(r0, c), sl] for sl in sls]
        decay_col, beta_col, gamma = [], [], []
        for h in heads:
            a_row = a_ref[h, pl.ds(ci, 1), :]
            b_row = b_ref[h, pl.ds(ci, 1), :]
            sp_in = a_row + dtb_ref[h]
            softplus = jnp.maximum(sp_in, 0.0) + jnp.log1p(jnp.exp(-jnp.abs(sp_in)))
            g_row = -jnp.exp(alog_ref[h]) * softplus
            beta_row = _sigmoid(b_row)
            g_b = jnp.broadcast_to(g_row, (c, c))
            dcol = jnp.sum(jnp.where(causal, g_b, 0.0), axis=1, keepdims=True)
            drow = jnp.sum(jnp.where(diag, jnp.broadcast_to(dcol, (c, c)), 0.0),
                           axis=0, keepdims=True)
            bcol = jnp.sum(jnp.where(diag, jnp.broadcast_to(beta_row, (c, c)), 0.0),
                           axis=1, keepdims=True)
            decay_col.append(dcol)
            beta_col.append(bcol)
            gamma.append(jnp.where(causal, jnp.exp(jnp.where(causal, dcol - drow, 0.0)), 0.0))
        k_beta = [kc[h] * beta_col[h] for h in heads]
        kq = [_bdot_nt(jnp.concatenate([k_beta[h], qc[h]], axis=0), kc[h]) for h in heads]
        qk = [kq[h][c:, :] * gamma[h] for h in heads]
        pw = [jnp.where(strict, -(kq[h][:c, :] * gamma[h]), 0.0) for h in heads]
        p = [eye + pw[h] for h in heads]
        pw = [_bdot(pw[h], pw[h]) for h in heads]
        span = 2
        while 2 * span < c:
            st = [_bdot(jnp.concatenate([pw[h], p[h]], axis=0), pw[h]) for h in heads]
            p = [p[h] + st[h][c:, :] for h in heads]
            pw = [st[h][:c, :] for h in heads]
            span *= 2
        t_inv = [p[h] + _bdot(p[h], pw[h]) for h in heads]
        e_col = [jnp.exp(decay_col[h]) for h in heads]
        uw = [_bdot(t_inv[h], jnp.concatenate([vc[h] * beta_col[h], k_beta[h] * e_col[h]], axis=1))
              for h in heads]
        s = [s_ref[h] for h in heads]
        ws = [_bdot(jnp.concatenate([uw[h][:, d:], qc[h] * e_col[h]], axis=0), s[h])
              for h in heads]
        v_new = [uw[h][:, :d] - ws[h][:c, :] for h in heads]
        decay_last = [decay_col[h][c - 1:c, :] for h in heads]
        k_tail = [kc[h] * jnp.exp(decay_last[h] - decay_col[h]) for h in heads]
        o_intra = [_bdot(qk[h], v_new[h]) for h in heads]
        kv = [_bdot_tn(k_tail[h], v_new[h]) for h in heads]
        for h in heads:
            s_ref[h] = s[h] * jnp.exp(decay_last[h]) + kv[h]
            o = ws[h][c:, :] + o_intra[h]
            o = o * lax.rsqrt(jnp.mean(o * o, axis=-1, keepdims=True) + RMS_EPS) * nw
            zc = z_ref[pl.ds(r0, c), sls[h]]
            o = o * (zc * _sigmoid(zc))
            o_ref[pl.ds(r0, c), sls[h]] = o.astype(o_ref.dtype)
        return carry

    lax.fori_loop(0, nc, chunk_body, 0)


def _gdn(proj, conv_w, a_t, b_t, a_log, dt_bias, norm_w, *, heads):
    seq = proj.shape[0]
    d = GDN_HEAD_DIM
    hb = _pick_tile(heads, (8, 6, 4, 2))
    tb = _pick_tile(seq, (512,))
    n_hb = heads // hb
    wblk = hb * d
    kernel = functools.partial(_gdn_kernel, tb=tb, hb=hb)

    def col(offset):
        return lambda h, t: (t, offset * n_hb + h)

    def wcol(offset):
        return lambda h, t: (0, offset * n_hb + h)

    return pl.pallas_call(
        kernel,
        out_shape=jax.ShapeDtypeStruct((seq, heads * d), BF16),
        grid=(n_hb, seq // tb),
        in_specs=[pl.BlockSpec((tb, wblk), col(0)),
                  pl.BlockSpec((tb, wblk), col(1)),
                  pl.BlockSpec((tb, wblk), col(2)),
                  pl.BlockSpec((tb, wblk), col(3)),
                  pl.BlockSpec((CONV_WIDTH, wblk), wcol(0)),
                  pl.BlockSpec((CONV_WIDTH, wblk), wcol(1)),
                  pl.BlockSpec((CONV_WIDTH, wblk), wcol(2)),
                  pl.BlockSpec((hb, tb // GDN_CHUNK, GDN_CHUNK), lambda h, t: (h, t, 0)),
                  pl.BlockSpec((hb, tb // GDN_CHUNK, GDN_CHUNK), lambda h, t: (h, t, 0)),
                  pl.BlockSpec((hb, 1, 1), lambda h, t: (h, 0, 0)),
                  pl.BlockSpec((hb, 1, 1), lambda h, t: (h, 0, 0)),
                  pl.BlockSpec((1, d), lambda h, t: (0, 0))],
        out_specs=pl.BlockSpec((tb, wblk), lambda h, t: (t, h)),
        scratch_shapes=[pltpu.VMEM((hb, d, d), F32),
                        pltpu.VMEM((tb + SUBLANES, wblk), F32),
                        pltpu.VMEM((tb + SUBLANES, wblk), F32),
                        pltpu.VMEM((tb + SUBLANES, wblk), F32),
                        pltpu.VMEM((tb, wblk), F32),
                        pltpu.VMEM((tb, wblk), F32),
                        pltpu.VMEM((tb, wblk), F32)],
        compiler_params=_params(("parallel", "arbitrary")),
        name="gdn",
    )(proj, proj, proj, proj, conv_w, conv_w, conv_w, a_t, b_t,
      a_log.reshape(heads, 1, 1), dt_bias.reshape(heads, 1, 1), norm_w.reshape(1, d))


def _gelu_tanh(y):
    return 0.5 * y * (1.0 + jnp.tanh(math.sqrt(2.0 / math.pi) * (y + 0.044715 * (y * y * y))))


def _s5_kernel(u_ref, bre_ref, bim_ref, cre_ref, cim_ref, are_ref, aim_ref, d_ref, wglu_ref,
               o_ref, xr_ref, xi_ref, str_ref, sti_ref, y_ref, *, tb, pitch, slabs):
    tiles = slabs * S5_TILES_PER_SLAB
    n_vreg = tiles // SUBLANES

    @pl.when(pl.program_id(0) == 0)
    def _():
        str_ref[...] = jnp.zeros_like(str_ref)
        sti_ref[...] = jnp.zeros_like(sti_ref)

    for s in range(slabs):
        ub = u_ref[:, s * LANES:(s + 1) * LANES].astype(BF16)
        r = jnp.dot(ub, bre_ref[s], preferred_element_type=F32)
        m = jnp.dot(ub, bim_ref[s], preferred_element_type=F32)
        for t4 in range(S5_TILES_PER_SLAB):
            lt = s * S5_TILES_PER_SLAB + t4
            xr_ref[lt * pitch:lt * pitch + tb, :] = r[:, t4 * LANES:(t4 + 1) * LANES]
            xi_ref[lt * pitch:lt * pitch + tb, :] = m[:, t4 * LANES:(t4 + 1) * LANES]

    ar = [are_ref[j] for j in range(n_vreg)]
    ai = [aim_ref[j] for j in range(n_vreg)]

    def step(t, carry):
        xr, xi = carry
        nxr, nxi = [], []
        for j in range(n_vreg):
            rows = pl.ds(j * SUBLANES * pitch + t, SUBLANES, stride=pitch)
            br = xr_ref[rows, :]
            bi = xi_ref[rows, :]
            r = ar[j] * xr[j] - ai[j] * xi[j] + br
            m = ar[j] * xi[j] + ai[j] * xr[j] + bi
            xr_ref[rows, :] = r
            xi_ref[rows, :] = m
            nxr.append(r)
            nxi.append(m)
        return tuple(nxr), tuple(nxi)

    x0 = (tuple(str_ref[j] for j in range(n_vreg)), tuple(sti_ref[j] for j in range(n_vreg)))
    xr_f, xi_f = lax.fori_loop(0, tb, step, x0)
    for j in range(n_vreg):
        str_ref[j] = xr_f[j]
        sti_ref[j] = xi_f[j]

    for s in range(slabs):
        acc = jnp.zeros((tb, LANES), F32)
        for t4 in range(S5_TILES_PER_SLAB):
            lt = s * S5_TILES_PER_SLAB + t4
            xr = xr_ref[lt * pitch:lt * pitch + tb, :].astype(BF16)
            xi = xi_ref[lt * pitch:lt * pitch + tb, :].astype(BF16)
            acc = acc + jnp.dot(xr, cre_ref[s, t4 * LANES:(t4 + 1) * LANES, :],
                                preferred_element_type=F32)
            acc = acc - jnp.dot(xi, cim_ref[s, t4 * LANES:(t4 + 1) * LANES, :],
                                preferred_element_type=F32)
        sl = slice(s * LANES, (s + 1) * LANES)
        y_ref[:, sl] = _gelu_tanh(acc + d_ref[:, sl] * u_ref[:, sl])

    y = y_ref[...]
    gate = jnp.dot(y.astype(BF16), wglu_ref[...], preferred_element_type=F32)
    o_ref[...] = (y * _sigmoid(gate)).astype(o_ref.dtype)


def _s5(proj, u_col_block, lam_re, lam_im, b_re, b_im, c_re, c_im, d_skip, log_dt, w_glu):
    seq = proj.shape[0]
    groups = lam_re.shape[0]
    width = groups * S5_GROUP
    slabs = width // LANES
    tiles = slabs * S5_TILES_PER_SLAB
    assert tiles % SUBLANES == 0
    tb = _pick_tile(seq, (512,))
    pitch = tb + SUBLANES

    dt = jnp.exp(log_dt)[:, None]
    mag = jnp.exp(lam_re * dt)
    ab_re, ab_im = mag * jnp.cos(lam_im * dt), mag * jnp.sin(lam_im * dt)
    den = lam_re * lam_re + lam_im * lam_im
    nr, ni = ab_re - 1.0, ab_im
    coef_re = (nr * lam_re + ni * lam_im) / den
    coef_im = (ni * lam_re - nr * lam_im) / den
    bb_re = coef_re[..., None] * b_re - coef_im[..., None] * b_im
    bb_im = coef_re[..., None] * b_im + coef_im[..., None] * b_re
    eye = jnp.eye(S5_GROUPS_PER_SLAB, dtype=F32)

    def b_blockdiag(bb):
        t = bb.reshape(slabs, S5_GROUPS_PER_SLAB, S5_STATE, S5_GROUP)
        m = jnp.einsum('saph,ab->sahbp', t, eye)
        return m.reshape(slabs, LANES, S5_SLAB_STATES).astype(BF16)

    def c_blockdiag(cc):
        t = cc.reshape(slabs, S5_GROUPS_PER_SLAB, S5_GROUP, S5_STATE)
        m = jnp.einsum('sahp,ab->sapbh', t, eye)
        return m.reshape(slabs, S5_SLAB_STATES, LANES).astype(BF16)

    n_vreg = tiles // SUBLANES
    kernel = functools.partial(_s5_kernel, tb=tb, pitch=pitch, slabs=slabs)
    full3 = lambda i: (0, 0, 0)
    return pl.pallas_call(
        kernel,
        out_shape=jax.ShapeDtypeStruct((seq, width), BF16),
        grid=(seq // tb,),
        in_specs=[pl.BlockSpec((tb, width), lambda i: (i, u_col_block)),
                  pl.BlockSpec((slabs, LANES, S5_SLAB_STATES), full3),
                  pl.BlockSpec((slabs, LANES, S5_SLAB_STATES), full3),
                  pl.BlockSpec((slabs, S5_SLAB_STATES, LANES), full3),
                  pl.BlockSpec((slabs, S5_SLAB_STATES, LANES), full3),
                  pl.BlockSpec((n_vreg, SUBLANES, LANES), full3),
                  pl.BlockSpec((n_vreg, SUBLANES, LANES), full3),
                  pl.BlockSpec((1, width), lambda i: (0, 0)),
                  pl.BlockSpec((width, width), lambda i: (0, 0))],
        out_specs=pl.BlockSpec((tb, width), lambda i: (i, 0)),
        scratch_shapes=[pltpu.VMEM((tiles * pitch, LANES), F32),
                        pltpu.VMEM((tiles * pitch, LANES), F32),
                        pltpu.VMEM((n_vreg, SUBLANES, LANES), F32),
                        pltpu.VMEM((n_vreg, SUBLANES, LANES), F32),
                        pltpu.VMEM((tb, width), F32)],
        compiler_params=_params(("arbitrary",)),
        name="s5",
    )(proj, b_blockdiag(bb_re), b_blockdiag(bb_im), c_blockdiag(c_re), c_blockdiag(c_im),
      ab_re.reshape(n_vreg, SUBLANES, LANES), ab_im.reshape(n_vreg, SUBLANES, LANES),
      d_skip.reshape(1, width), w_glu.astype(BF16))


def _ln_rows(x_ref, m_ref, g_ref, b_ref, alpha):
    h = alpha * x_ref[...] + m_ref[...]
    mu = jnp.mean(h, axis=-1, keepdims=True)
    hc = h - mu
    var = jnp.mean(hc * hc, axis=-1, keepdims=True)
    return hc * lax.rsqrt(var + LN_EPS) * g_ref[...] + b_ref[...]


def _ln_kernel(x_ref, m_ref, g_ref, b_ref, o_ref, ob_ref, *, alpha):
    y = _ln_rows(x_ref, m_ref, g_ref, b_ref, alpha)
    o_ref[...] = y
    ob_ref[...] = y.astype(BF16)


def _ln_router_kernel(x_ref, m_ref, g_ref, b_ref, wr_ref, o_ref, ob_ref, lg_ref, *, alpha):
    y = _ln_rows(x_ref, m_ref, g_ref, b_ref, alpha)
    o_ref[...] = y
    ob_ref[...] = y.astype(BF16)
    lg_ref[...] = jnp.dot(y, wr_ref[...], preferred_element_type=F32,
                          precision=lax.Precision.HIGHEST)


def _residual_layer_norm(x, mix, g, b, alpha, w_router=None):
    seq, dm = x.shape
    tm = _pick_tile(seq, (256, 128))
    row = pl.BlockSpec((tm, dm), lambda i: (i, 0))
    vec = pl.BlockSpec((1, dm), lambda i: (0, 0))
    out_shape = [jax.ShapeDtypeStruct((seq, dm), F32), jax.ShapeDtypeStruct((seq, dm), BF16)]
    out_specs = [row, row]
    args = [x, mix, g.reshape(1, dm), b.reshape(1, dm)]
    in_specs = [row, row, vec, vec]
    if w_router is None:
        kernel = functools.partial(_ln_kernel, alpha=alpha)
    else:
        kernel = functools.partial(_ln_router_kernel, alpha=alpha)
        nr = w_router.shape[1]
        args.append(w_router)
        in_specs.append(pl.BlockSpec((dm, nr), lambda i: (0, 0)))
        out_shape.append(jax.ShapeDtypeStruct((seq, nr), F32))
        out_specs.append(pl.BlockSpec((tm, nr), lambda i: (i, 0)))
    return pl.pallas_call(
        kernel,
        out_shape=out_shape,
        grid=(seq // tm,),
        in_specs=in_specs,
        out_specs=out_specs,
        compiler_params=_params(("parallel",)),
        name="residual_layer_norm",
    )(*args)


def _moe_kernel(be_ref, x_ref, rw_ref, wg_ref, wu_ref, wd_ref, o_ref):
    del be_ref
    x = x_ref[...]
    g = jnp.dot(x, wg_ref[0], preferred_element_type=F32)
    u = jnp.dot(x, wu_ref[0], preferred_element_type=F32)
    hid = (g * _sigmoid(g)) * u
    y = jnp.dot(hid.astype(BF16), wd_ref[0], preferred_element_type=F32)
    o_ref[...] = y * rw_ref[...]


def _moe_experts(xb, row_w, blk_expert, w_gate, w_up, w_down, *, bm):
    n_rows, dm = xb.shape
    de = w_gate.shape[2]
    n_blocks = n_rows // bm
    grid_spec = pltpu.PrefetchScalarGridSpec(
        num_scalar_prefetch=1,
        grid=(n_blocks,),
        in_specs=[pl.BlockSpec((bm, dm), lambda i, be: (i, 0)),
                  pl.BlockSpec((bm, 1), lambda i, be: (i, 0)),
                  pl.BlockSpec((1, dm, de), lambda i, be: (be[i], 0, 0)),
                  pl.BlockSpec((1, dm, de), lambda i, be: (be[i], 0, 0)),
                  pl.BlockSpec((1, de, dm), lambda i, be: (be[i], 0, 0))],
        out_specs=pl.BlockSpec((bm, dm), lambda i, be: (i, 0)),
    )
    return pl.pallas_call(
        _moe_kernel,
        out_shape=jax.ShapeDtypeStruct((n_rows, dm), F32),
        grid_spec=grid_spec,
        compiler_params=_params(("arbitrary",)),
        name="moe_experts",
    )(blk_expert, xb, row_w, w_gate, w_up, w_down)


def _moe(x_bf, logits, b_rg, b_re, w_gate, w_up, w_down, *, bm):
    n_tok, dm = x_bf.shape
    n_experts = N_GROUPS * EXPERTS_PER_GROUP
    grp_logits = logits[:, :N_GROUPS] + b_rg
    exp_logits = logits[:, N_GROUPS:N_GROUPS + n_experts].reshape(n_tok, N_GROUPS, EXPERTS_PER_GROUP) + b_re
    grp_p, grp_idx = lax.top_k(jax.nn.softmax(grp_logits, axis=-1), 1)
    sel = exp_logits[jnp.arange(n_tok), grp_idx[:, 0]]
    top_logit, top_idx = lax.top_k(sel, TOP_K)
    top_w = jax.nn.softmax(top_logit, axis=-1) * grp_p
    expert_id = (grp_idx * EXPERTS_PER_GROUP + top_idx).reshape(-1).astype(jnp.int32)
    n_assign = n_tok * TOP_K
    tok_id = jnp.arange(n_assign, dtype=jnp.int32) // TOP_K
    wts = top_w.reshape(-1)
    order = jnp.argsort(expert_id)
    e_sorted = expert_id[order]
    counts = jnp.zeros((n_experts,), jnp.int32).at[expert_id].add(1)
    start = jnp.cumsum(counts) - counts
    padded = (counts + bm - 1) // bm * bm
    pend = jnp.cumsum(padded)
    pstart = pend - padded
    dest = pstart[e_sorted] + (jnp.arange(n_assign, dtype=jnp.int32) - start[e_sorted])
    n_blocks = (n_assign + bm - 1) // bm + n_experts
    n_rows = n_blocks * bm
    row_tok = jnp.zeros((n_rows,), jnp.int32).at[dest].set(tok_id[order])
    row_w = jnp.zeros((n_rows,), F32).at[dest].set(wts[order])
    blk_start = jnp.arange(n_blocks, dtype=jnp.int32) * bm
    blk_expert = jnp.minimum(jnp.searchsorted(pend, blk_start, side='right'),
                             n_experts - 1).astype(jnp.int32)
    xb = x_bf[row_tok]
    yb = _moe_experts(xb, row_w[:, None], blk_expert, w_gate, w_up, w_down, bm=bm)
    pos = jnp.zeros((n_assign,), jnp.int32).at[order].set(dest).reshape(n_tok, TOP_K)
    return yb[pos[:, 0]] + yb[pos[:, 1]]


def kernel(x, w_in, gdn_conv_w, gdn_a_log, gdn_dt_bias, gdn_norm_w, s5_lambda_re, s5_lambda_im, s5_b_re, s5_b_im, s5_c_re, s5_c_im, s5_d, s5_log_dt, s5_w_glu, w_out, ln1_g, ln1_b, router_group_w, router_group_b, router_expert_w, router_expert_b, expert_w_gate, expert_w_up, expert_w_down, ln2_g, ln2_b):
    bsz, seq, dm = x.shape
    assert bsz == 1
    depth = w_in.shape[0]
    heads = gdn_a_log.shape[1]
    gdn_width = heads * GDN_HEAD_DIM
    s5_width = s5_d.shape[1]
    c_z = 4 * gdn_width
    c_b = c_z + 2 * heads
    assert (4 * gdn_width) % s5_width == 0
    alpha = (2 * depth) ** 0.25
    n_experts = N_GROUPS * EXPERTS_PER_GROUP
    router_cols = LANES
    moe_bm = 256

    xf = x.reshape(seq, dm)
    xb = xf.astype(BF16)
    for i in range(depth):
        w_main = jnp.concatenate([w_in[i][:, :c_z], w_in[i][:, c_b:]], axis=1).astype(BF16)
        w_ab = jnp.pad(w_in[i][:, c_z:c_b], ((0, 0), (0, LANES - 2 * heads))).astype(BF16)
        proj = _matmul(xb, w_main, F32)
        proj_ab = _matmul(xb, w_ab, F32)
        a_t = proj_ab[:, :heads].T.reshape(heads, seq // GDN_CHUNK, GDN_CHUNK)
        b_t = proj_ab[:, heads:2 * heads].T.reshape(heads, seq // GDN_CHUNK, GDN_CHUNK)
        y_gdn = _gdn(proj, gdn_conv_w[i], a_t, b_t, gdn_a_log[i], gdn_dt_bias[i], gdn_norm_w[i],
                     heads=heads)
        y_s5 = _s5(proj, c_z // s5_width, s5_lambda_re[i], s5_lambda_im[i], s5_b_re[i], s5_b_im[i],
                   s5_c_re[i], s5_c_im[i], s5_d[i], s5_log_dt[i], s5_w_glu[i])
        mix = _matmul(jnp.concatenate([y_gdn, y_s5], axis=1), w_out[i].astype(BF16), F32)
        w_router = jnp.concatenate(
            [router_group_w[i],
             jnp.transpose(router_expert_w[i], (1, 0, 2)).reshape(dm, n_experts),
             jnp.zeros((dm, router_cols - N_GROUPS - n_experts), F32)], axis=1)
        xf, xb, logits = _residual_layer_norm(xf, mix, ln1_g[i], ln1_b[i], alpha, w_router)
        ffn = _moe(xb, logits, router_group_b[i], router_expert_b[i],
                   expert_w_gate[i].astype(BF16), expert_w_up[i].astype(BF16),
                   expert_w_down[i].astype(BF16), bm=moe_bm)
        xf, xb = _residual_layer_norm(xf, ffn, ln2_g[i], ln2_b[i], alpha)
    return xf.reshape(bsz, seq, dm)
```

```python
import functools
import math

import jax
import jax.numpy as jnp
from jax import lax
from jax.experimental import pallas as pl
from jax.experimental.pallas import tpu as pltpu

F32 = jnp.float32
BF16 = jnp.bfloat16

LANES = 128
SUBLANES = 8
VMEM_LIMIT_BYTES = 56 * 1024 * 1024

GDN_CHUNK = 64
GDN_HEAD_DIM = 128
CONV_WIDTH = 4
S5_GROUP = 16
S5_STATE = 64
S5_GROUPS_PER_SLAB = LANES // S5_GROUP
S5_SLAB_STATES = S5_GROUPS_PER_SLAB * S5_STATE
S5_TILES_PER_SLAB = S5_SLAB_STATES // LANES
N_GROUPS = 4
EXPERTS_PER_GROUP = 8
TOP_K = 2
LN_EPS = 1e-5
RMS_EPS = 1e-6
L2_EPS = 1e-6


def _params(sem):
    return pltpu.CompilerParams(dimension_semantics=sem, vmem_limit_bytes=VMEM_LIMIT_BYTES)


def _pick_tile(n, candidates):
    for c in candidates:
        if n % c == 0:
            return c
    return n


def _bdot(a, b):
    return jnp.dot(a.astype(BF16), b.astype(BF16), preferred_element_type=F32)


def _bdot_nt(a, b):
    return lax.dot_general(a.astype(BF16), b.astype(BF16), (((1,), (1,)), ((), ())),
                           preferred_element_type=F32)


def _bdot_tn(a, b):
    return lax.dot_general(a.astype(BF16), b.astype(BF16), (((0,), (0,)), ((), ())),
                           preferred_element_type=F32)


def _sigmoid(x):
    return 1.0 / (1.0 + jnp.exp(-x))


def _cast_rows_to_bf16(src_ref, dst_ref, rows_per_step):
    def body(r, carry):
        rows = pl.ds(pl.multiple_of(r * rows_per_step, rows_per_step), rows_per_step)
        dst_ref[rows, :] = src_ref[rows, :].astype(BF16)
        return carry

    lax.fori_loop(0, src_ref.shape[0] // rows_per_step, body, 0)


def _mm_kernel(a_ref, b_ref, o_ref, bb_ref, *, cast_rows):
    @pl.when(pl.program_id(1) == 0)
    def _():
        _cast_rows_to_bf16(b_ref, bb_ref, cast_rows)

    o_ref[...] = jnp.dot(a_ref[...], bb_ref[...], preferred_element_type=F32).astype(o_ref.dtype)


def _matmul(a, w, layer, n_cols, out_dtype):
    m, kdim = a.shape
    tm = _pick_tile(m, (1024, 512, 256, 128))
    tn = _pick_tile(n_cols, (512, 384, 256, 128))
    cast_rows = _pick_tile(kdim, (512, 256, 128))
    return pl.pallas_call(
        functools.partial(_mm_kernel, cast_rows=cast_rows),
        out_shape=jax.ShapeDtypeStruct((m, n_cols), out_dtype),
        grid=(n_cols // tn, m // tm),
        in_specs=[pl.BlockSpec((tm, kdim), lambda j, i: (i, 0)),
                  pl.BlockSpec((None, kdim, tn), lambda j, i: (layer, 0, j))],
        out_specs=pl.BlockSpec((tm, tn), lambda j, i: (i, j)),
        scratch_shapes=[pltpu.VMEM((kdim, tn), BF16)],
        compiler_params=_params(("arbitrary", "arbitrary")),
        name="matmul",
    )(a, w)


def _gdn_kernel(q_ref, k_ref, v_ref, z_ref, wq_ref, wk_ref, wv_ref, a_ref, b_ref,
                alog_ref, dtb_ref, nw_ref, o_ref,
                s_ref, cq_ref, ck_ref, cv_ref, qs_ref, ks_ref, vs_ref, *, tb, hb):
    c = GDN_CHUNK
    d = GDN_HEAD_DIM
    nc = tb // c

    @pl.when(pl.program_id(1) == 0)
    def _():
        s_ref[...] = jnp.zeros_like(s_ref)
        cq_ref[...] = jnp.zeros_like(cq_ref)
        ck_ref[...] = jnp.zeros_like(ck_ref)
        cv_ref[...] = jnp.zeros_like(cv_ref)

    def conv_body(ri, carry):
        r0 = pl.multiple_of(ri * c, c)
        rp = pl.multiple_of(jnp.maximum(r0 - SUBLANES, 0), SUBLANES)
        for x_ref, c_ref, w_ref, dst_ref, scale in ((q_ref, cq_ref, wq_ref, qs_ref, d ** -0.5),
                                                    (k_ref, ck_ref, wk_ref, ks_ref, 1.0),
                                                    (v_ref, cv_ref, wv_ref, vs_ref, None)):
            for h in range(hb):
                sl = slice(h * d, (h + 1) * d)
                prev = jnp.where(ri == 0, c_ref[:, sl], x_ref[pl.ds(rp, SUBLANES), sl])
                ext = jnp.concatenate([prev, x_ref[pl.ds(r0, c), sl]], axis=0)
                w = w_ref[:, sl]
                acc = ext[SUBLANES:, :] * w[CONV_WIDTH - 1:CONV_WIDTH, :]
                for j in range(1, CONV_WIDTH):
                    acc = acc + ext[SUBLANES - j:SUBLANES - j + c, :] * w[CONV_WIDTH - 1 - j:CONV_WIDTH - j, :]
                y = acc * _sigmoid(acc)
                if scale is not None:
                    y = y * (lax.rsqrt(jnp.sum(y * y, axis=-1, keepdims=True) + L2_EPS) * scale)
                dst_ref[pl.ds(r0, c), sl] = y
        return carry

    lax.fori_loop(0, nc, conv_body, 0)
    cq_ref[...] = q_ref[tb - SUBLANES:tb, :]
    ck_ref[...] = k_ref[tb - SUBLANES:tb, :]
    cv_ref[...] = v_ref[tb - SUBLANES:tb, :]

    ii = lax.broadcasted_iota(jnp.int32, (c, c), 0)
    jj = lax.broadcasted_iota(jnp.int32, (c, c), 1)
    causal = jj <= ii
    strict = jj < ii
    diag = jj == ii
    nw = nw_ref[...]

    eye = jnp.where(diag, 1.0, 0.0).astype(F32)
    heads = range(hb)

    def chunk_body(ci, carry):
        r0 = pl.multiple_of(ci * c, c)
        sls = [slice(h * d, (h + 1) * d) for h in heads]
        qc = [qs_ref[pl.ds(r0, c), sl] for sl in sls]
        kc = [ks_ref[pl.ds(r0, c), sl] for sl in sls]
        vc = [vs_ref[pl.ds(r0, c), sl] for sl in sls]
        decay_col, beta_col, gamma = [], [], []
        for h in heads:
            a_row = a_ref[h, pl.ds(ci, 1), :]
            b_row = b_ref[h, pl.ds(ci, 1), :]
            sp_in = a_row + dtb_ref[h]
            softplus = jnp.maximum(sp_in, 0.0) + jnp.log1p(jnp.exp(-jnp.abs(sp_in)))
            g_row = -jnp.exp(alog_ref[h]) * softplus
            beta_row = _sigmoid(b_row)
            g_b = jnp.broadcast_to(g_row, (c, c))
            dcol = jnp.sum(jnp.where(causal, g_b, 0.0), axis=1, keepdims=True)
            drow = jnp.sum(jnp.where(diag, jnp.broadcast_to(dcol, (c, c)), 0.0),
                           axis=0, keepdims=True)
            bcol = jnp.sum(jnp.where(diag, jnp.broadcast_to(beta_row, (c, c)), 0.0),
                           axis=1, keepdims=True)
            decay_col.append(dcol)
            beta_col.append(bcol)
            gamma.append(jnp.where(causal, jnp.exp(jnp.where(causal, dcol - drow, 0.0)), 0.0))
        k_beta = [kc[h] * beta_col[h] for h in heads]
        kq = [_bdot_nt(jnp.concatenate([k_beta[h], qc[h]], axis=0), kc[h]) for h in heads]
        qk = [kq[h][c:, :] * gamma[h] for h in heads]
        pw = [jnp.where(strict, -(kq[h][:c, :] * gamma[h]), 0.0) for h in heads]
        p = [eye + pw[h] for h in heads]
        pw = [_bdot(pw[h], pw[h]) for h in heads]
        span = 2
        while 2 * span < c:
            st = [_bdot(jnp.concatenate([pw[h], p[h]], axis=0), pw[h]) for h in heads]
            p = [p[h] + st[h][c:, :] for h in heads]
            pw = [st[h][:c, :] for h in heads]
            span *= 2
        t_inv = [p[h] + _bdot(p[h], pw[h]) for h in heads]
        e_col = [jnp.exp(decay_col[h]) for h in heads]
        uw = [_bdot(t_inv[h], jnp.concatenate([vc[h] * beta_col[h], k_beta[h] * e_col[h]], axis=1))
              for h in heads]
        s = [s_ref[h] for h in heads]
        ws = [_bdot(jnp.concatenate([uw[h][:, d:], qc[h] * e_col[h]], axis=0), s[h])
              for h in heads]
        v_new = [uw[h][:, :d] - ws[h][:c, :] for h in heads]
        decay_last = [decay_col[h][c - 1:c, :] for h in heads]
        k_tail = [kc[h] * jnp.exp(decay_last[h] - decay_col[h]) for h in heads]
        o_intra = [_bdot(qk[h], v_new[h]) for h in heads]
        kv = [_bdot_tn(k_tail[h], v_new[h]) for h in heads]
        for h in heads:
            s_ref[h] = s[h] * jnp.exp(decay_last[h]) + kv[h]
            o = ws[h][c:, :] + o_intra[h]
            o = o * lax.rsqrt(jnp.mean(o * o, axis=-1, keepdims=True) + RMS_EPS) * nw
            zc = z_ref[pl.ds(r0, c), sls[h]]
            o = o * (zc * _sigmoid(zc))
            o_ref[pl.ds(r0, c), sls[h]] = o.astype(o_ref.dtype)
        return carry

    lax.fori_loop(0, nc, chunk_body, 0)


def _gdn(proj, conv_w, a_t, b_t, a_log, dt_bias, norm_w, *, heads):
    seq = proj.shape[0]
    d = GDN_HEAD_DIM
    hb = _pick_tile(heads, (8, 6, 4, 2))
    tb = _pick_tile(seq, (512,))
    n_hb = heads // hb
    wblk = hb * d
    kernel = functools.partial(_gdn_kernel, tb=tb, hb=hb)

    def col(offset):
        return lambda h, t: (t, offset * n_hb + h)

    def wcol(offset):
        return lambda h, t: (0, offset * n_hb + h)

    return pl.pallas_call(
        kernel,
        out_shape=jax.ShapeDtypeStruct((seq, heads * d), BF16),
        grid=(n_hb, seq // tb),
        in_specs=[pl.BlockSpec((tb, wblk), col(0)),
                  pl.BlockSpec((tb, wblk), col(1)),
                  pl.BlockSpec((tb, wblk), col(2)),
                  pl.BlockSpec((tb, wblk), col(3)),
                  pl.BlockSpec((CONV_WIDTH, wblk), wcol(0)),
                  pl.BlockSpec((CONV_WIDTH, wblk), wcol(1)),
                  pl.BlockSpec((CONV_WIDTH, wblk), wcol(2)),
                  pl.BlockSpec((hb, tb // GDN_CHUNK, GDN_CHUNK), lambda h, t: (h, t, 0)),
                  pl.BlockSpec((hb, tb // GDN_CHUNK, GDN_CHUNK), lambda h, t: (h, t, 0)),
                  pl.BlockSpec((hb, 1, 1), lambda h, t: (h, 0, 0)),
                  pl.BlockSpec((hb, 1, 1), lambda h, t: (h, 0, 0)),
                  pl.BlockSpec((1, d), lambda h, t: (0, 0))],
        out_specs=pl.BlockSpec((tb, wblk), lambda h, t: (t, h)),
        scratch_shapes=[pltpu.VMEM((hb, d, d), F32),
                        pltpu.VMEM((SUBLANES, wblk), F32),
                        pltpu.VMEM((SUBLANES, wblk), F32),
                        pltpu.VMEM((SUBLANES, wblk), F32),
                        pltpu.VMEM((tb, wblk), F32),
                        pltpu.VMEM((tb, wblk), F32),
                        pltpu.VMEM((tb, wblk), F32)],
        compiler_params=_params(("parallel", "arbitrary")),
        name="gdn",
    )(proj, proj, proj, proj, conv_w, conv_w, conv_w, a_t, b_t,
      a_log.reshape(heads, 1, 1), dt_bias.reshape(heads, 1, 1), norm_w.reshape(1, d))


def _gelu_tanh(y):
    return 0.5 * y * (1.0 + jnp.tanh(math.sqrt(2.0 / math.pi) * (y + 0.044715 * (y * y * y))))


def _s5_kernel(u_ref, bre_ref, bim_ref, cre_ref, cim_ref, are_ref, aim_ref, d_ref, wglu_ref,
               o_ref, xr_ref, xi_ref, str_ref, sti_ref, y_ref, wglu_bf_ref, *, tb, pitch, slabs):
    tiles = slabs * S5_TILES_PER_SLAB
    n_vreg = tiles // SUBLANES

    @pl.when(pl.program_id(0) == 0)
    def _():
        str_ref[...] = jnp.zeros_like(str_ref)
        sti_ref[...] = jnp.zeros_like(sti_ref)
        _cast_rows_to_bf16(wglu_ref, wglu_bf_ref, LANES)

    for s in range(slabs):
        ub = u_ref[:, s * LANES:(s + 1) * LANES].astype(BF16)
        r = jnp.dot(ub, bre_ref[s], preferred_element_type=F32)
        m = jnp.dot(ub, bim_ref[s], preferred_element_type=F32)
        for t4 in range(S5_TILES_PER_SLAB):
            lt = s * S5_TILES_PER_SLAB + t4
            xr_ref[lt * pitch:lt * pitch + tb, :] = r[:, t4 * LANES:(t4 + 1) * LANES]
            xi_ref[lt * pitch:lt * pitch + tb, :] = m[:, t4 * LANES:(t4 + 1) * LANES]

    ar = [are_ref[j] for j in range(n_vreg)]
    ai = [aim_ref[j] for j in range(n_vreg)]

    def step(t, carry):
        xr, xi = carry
        nxr, nxi = [], []
        for j in range(n_vreg):
            rows = pl.ds(j * SUBLANES * pitch + t, SUBLANES, stride=pitch)
            br = xr_ref[rows, :]
            bi = xi_ref[rows, :]
            r = ar[j] * xr[j] - ai[j] * xi[j] + br
            m = ar[j] * xi[j] + ai[j] * xr[j] + bi
            xr_ref[rows, :] = r
            xi_ref[rows, :] = m
            nxr.append(r)
            nxi.append(m)
        return tuple(nxr), tuple(nxi)

    x0 = (tuple(str_ref[j] for j in range(n_vreg)), tuple(sti_ref[j] for j in range(n_vreg)))
    xr_f, xi_f = lax.fori_loop(0, tb, step, x0)
    for j in range(n_vreg):
        str_ref[j] = xr_f[j]
        sti_ref[j] = xi_f[j]

    for s in range(slabs):
        acc = jnp.zeros((tb, LANES), F32)
        for t4 in range(S5_TILES_PER_SLAB):
            lt = s * S5_TILES_PER_SLAB + t4
            xr = xr_ref[lt * pitch:lt * pitch + tb, :].astype(BF16)
            xi = xi_ref[lt * pitch:lt * pitch + tb, :].astype(BF16)
            acc = acc + jnp.dot(xr, cre_ref[s, t4 * LANES:(t4 + 1) * LANES, :],
                                preferred_element_type=F32)
            acc = acc - jnp.dot(xi, cim_ref[s, t4 * LANES:(t4 + 1) * LANES, :],
                                preferred_element_type=F32)
        sl = slice(s * LANES, (s + 1) * LANES)
        y_ref[:, sl] = _gelu_tanh(acc + d_ref[:, sl] * u_ref[:, sl])

    y = y_ref[...]
    gate = jnp.dot(y.astype(BF16), wglu_bf_ref[...], preferred_element_type=F32)
    o_ref[...] = (y * _sigmoid(gate)).astype(o_ref.dtype)


def _s5(proj, u_col_block, lam_re, lam_im, b_re, b_im, c_re, c_im, d_skip, log_dt, w_glu):
    seq = proj.shape[0]
    groups = lam_re.shape[0]
    width = groups * S5_GROUP
    slabs = width // LANES
    tiles = slabs * S5_TILES_PER_SLAB
    assert tiles % SUBLANES == 0
    tb = _pick_tile(seq, (512,))
    pitch = tb + SUBLANES

    dt = jnp.exp(log_dt)[:, None]
    mag = jnp.exp(lam_re * dt)
    ab_re, ab_im = mag * jnp.cos(lam_im * dt), mag * jnp.sin(lam_im * dt)
    den = lam_re * lam_re + lam_im * lam_im
    nr, ni = ab_re - 1.0, ab_im
    coef_re = (nr * lam_re + ni * lam_im) / den
    coef_im = (ni * lam_re - nr * lam_im) / den
    bb_re = coef_re[..., None] * b_re - coef_im[..., None] * b_im
    bb_im = coef_re[..., None] * b_im + coef_im[..., None] * b_re
    eye = jnp.eye(S5_GROUPS_PER_SLAB, dtype=F32)

    def b_blockdiag(bb):
        t = bb.reshape(slabs, S5_GROUPS_PER_SLAB, S5_STATE, S5_GROUP)
        m = jnp.einsum('saph,ab->sahbp', t, eye)
        return m.reshape(slabs, LANES, S5_SLAB_STATES).astype(BF16)

    def c_blockdiag(cc):
        t = cc.reshape(slabs, S5_GROUPS_PER_SLAB, S5_GROUP, S5_STATE)
        m = jnp.einsum('sahp,ab->sapbh', t, eye)
        return m.reshape(slabs, S5_SLAB_STATES, LANES).astype(BF16)

    n_vreg = tiles // SUBLANES
    kernel = functools.partial(_s5_kernel, tb=tb, pitch=pitch, slabs=slabs)
    full3 = lambda i: (0, 0, 0)
    return pl.pallas_call(
        kernel,
        out_shape=jax.ShapeDtypeStruct((seq, width), BF16),
        grid=(seq // tb,),
        in_specs=[pl.BlockSpec((tb, width), lambda i: (i, u_col_block)),
                  pl.BlockSpec((slabs, LANES, S5_SLAB_STATES), full3),
                  pl.BlockSpec((slabs, LANES, S5_SLAB_STATES), full3),
                  pl.BlockSpec((slabs, S5_SLAB_STATES, LANES), full3),
                  pl.BlockSpec((slabs, S5_SLAB_STATES, LANES), full3),
                  pl.BlockSpec((n_vreg, SUBLANES, LANES), full3),
                  pl.BlockSpec((n_vreg, SUBLANES, LANES), full3),
                  pl.BlockSpec((1, width), lambda i: (0, 0)),
                  pl.BlockSpec((width, width), lambda i: (0, 0))],
        out_specs=pl.BlockSpec((tb, width), lambda i: (i, 0)),
        scratch_shapes=[pltpu.VMEM((tiles * pitch, LANES), F32),
                        pltpu.VMEM((tiles * pitch, LANES), F32),
                        pltpu.VMEM((n_vreg, SUBLANES, LANES), F32),
                        pltpu.VMEM((n_vreg, SUBLANES, LANES), F32),
                        pltpu.VMEM((tb, width), F32),
                        pltpu.VMEM((width, width), BF16)],
        compiler_params=_params(("arbitrary",)),
        name="s5",
    )(proj, b_blockdiag(bb_re), b_blockdiag(bb_im), c_blockdiag(c_re), c_blockdiag(c_im),
      ab_re.reshape(n_vreg, SUBLANES, LANES), ab_im.reshape(n_vreg, SUBLANES, LANES),
      d_skip.reshape(1, width), w_glu)


def _ln_rows(x_ref, m_ref, g_ref, b_ref, alpha):
    h = alpha * x_ref[...] + m_ref[...]
    mu = jnp.mean(h, axis=-1, keepdims=True)
    hc = h - mu
    var = jnp.mean(hc * hc, axis=-1, keepdims=True)
    return hc * lax.rsqrt(var + LN_EPS) * g_ref[...] + b_ref[...]


def _ln_kernel(x_ref, m_ref, g_ref, b_ref, o_ref, ob_ref, *, alpha):
    y = _ln_rows(x_ref, m_ref, g_ref, b_ref, alpha)
    o_ref[...] = y
    ob_ref[...] = y.astype(BF16)


def _ln_router_kernel(x_ref, m_ref, g_ref, b_ref, wr_ref, o_ref, ob_ref, lg_ref, *, alpha):
    y = _ln_rows(x_ref, m_ref, g_ref, b_ref, alpha)
    o_ref[...] = y
    ob_ref[...] = y.astype(BF16)
    lg_ref[...] = jnp.dot(y, wr_ref[...], preferred_element_type=F32,
                          precision=lax.Precision.HIGHEST)


def _residual_layer_norm(x, mix, g, b, alpha, w_router=None):
    seq, dm = x.shape
    tm = _pick_tile(seq, (256, 128))
    row = pl.BlockSpec((tm, dm), lambda i: (i, 0))
    vec = pl.BlockSpec((1, dm), lambda i: (0, 0))
    out_shape = [jax.ShapeDtypeStruct((seq, dm), F32), jax.ShapeDtypeStruct((seq, dm), BF16)]
    out_specs = [row, row]
    args = [x, mix, g.reshape(1, dm), b.reshape(1, dm)]
    in_specs = [row, row, vec, vec]
    if w_router is None:
        kernel = functools.partial(_ln_kernel, alpha=alpha)
    else:
        kernel = functools.partial(_ln_router_kernel, alpha=alpha)
        nr = w_router.shape[1]
        args.append(w_router)
        in_specs.append(pl.BlockSpec((dm, nr), lambda i: (0, 0)))
        out_shape.append(jax.ShapeDtypeStruct((seq, nr), F32))
        out_specs.append(pl.BlockSpec((tm, nr), lambda i: (i, 0)))
    return pl.pallas_call(
        kernel,
        out_shape=out_shape,
        grid=(seq // tm,),
        in_specs=in_specs,
        out_specs=out_specs,
        compiler_params=_params(("parallel",)),
        name="residual_layer_norm",
    )(*args)


def _moe_kernel(be_ref, first_ref, nxt_ref, x_ref, rw_ref, wg_hbm, wu_hbm, wd_hbm, o_ref,
                sg_ref, su_ref, sd_ref, bg_ref, bu_ref, bd_ref, sem, *, layer):
    b = pl.program_id(0)

    def weight_copies(e):
        return (pltpu.make_async_copy(wg_hbm.at[layer, e], sg_ref, sem.at[0]),
                pltpu.make_async_copy(wu_hbm.at[layer, e], su_ref, sem.at[1]),
                pltpu.make_async_copy(wd_hbm.at[layer, e], sd_ref, sem.at[2]))

    @pl.when(b == 0)
    def _():
        for cp in weight_copies(be_ref[0]):
            cp.start()

    @pl.when(first_ref[b] == 1)
    def _():
        for cp in weight_copies(be_ref[b]):
            cp.wait()
        _cast_rows_to_bf16(sg_ref, bg_ref, 512)
        _cast_rows_to_bf16(su_ref, bu_ref, 512)
        _cast_rows_to_bf16(sd_ref, bd_ref, 64)

        @pl.when(nxt_ref[b] >= 0)
        def _():
            for cp in weight_copies(nxt_ref[b]):
                cp.start()

    x = x_ref[...]
    g = jnp.dot(x, bg_ref[...], preferred_element_type=F32)
    u = jnp.dot(x, bu_ref[...], preferred_element_type=F32)
    hid = (g * _sigmoid(g)) * u
    y = jnp.dot(hid.astype(BF16), bd_ref[...], preferred_element_type=F32)
    o_ref[...] = y * rw_ref[...]


def _moe_experts(xb, row_w, blk_expert, blk_first, blk_next, w_gate, w_up, w_down, layer, *, bm):
    n_rows, dm = xb.shape
    de = w_gate.shape[3]
    n_blocks = n_rows // bm
    assert dm % 512 == 0 and de % 64 == 0
    grid_spec = pltpu.PrefetchScalarGridSpec(
        num_scalar_prefetch=3,
        grid=(n_blocks,),
        in_specs=[pl.BlockSpec((bm, dm), lambda i, be, fi, nx: (i, 0)),
                  pl.BlockSpec((bm, 1), lambda i, be, fi, nx: (i, 0)),
                  pl.BlockSpec(memory_space=pl.ANY),
                  pl.BlockSpec(memory_space=pl.ANY),
                  pl.BlockSpec(memory_space=pl.ANY)],
        out_specs=pl.BlockSpec((bm, dm), lambda i, be, fi, nx: (i, 0)),
        scratch_shapes=[pltpu.VMEM((dm, de), F32), pltpu.VMEM((dm, de), F32),
                        pltpu.VMEM((de, dm), F32),
                        pltpu.VMEM((dm, de), BF16), pltpu.VMEM((dm, de), BF16),
                        pltpu.VMEM((de, dm), BF16),
                        pltpu.SemaphoreType.DMA((3,))],
    )
    return pl.pallas_call(
        functools.partial(_moe_kernel, layer=layer),
        out_shape=jax.ShapeDtypeStruct((n_rows, dm), F32),
        grid_spec=grid_spec,
        compiler_params=_params(("arbitrary",)),
        name="moe_experts",
    )(blk_expert, blk_first, blk_next, xb, row_w, w_gate, w_up, w_down)


def _moe(x_bf, logits, b_rg, b_re, w_gate, w_up, w_down, layer, *, bm):
    n_tok, dm = x_bf.shape
    n_experts = N_GROUPS * EXPERTS_PER_GROUP
    grp_logits = logits[:, :N_GROUPS] + b_rg
    exp_logits = logits[:, N_GROUPS:N_GROUPS + n_experts].reshape(n_tok, N_GROUPS, EXPERTS_PER_GROUP) + b_re
    grp_p, grp_idx = lax.top_k(jax.nn.softmax(grp_logits, axis=-1), 1)
    sel = exp_logits[jnp.arange(n_tok), grp_idx[:, 0]]
    top_logit, top_idx = lax.top_k(sel, TOP_K)
    top_w = jax.nn.softmax(top_logit, axis=-1) * grp_p
    expert_id = (grp_idx * EXPERTS_PER_GROUP + top_idx).reshape(-1).astype(jnp.int32)
    n_assign = n_tok * TOP_K
    wts = top_w.reshape(-1)
    onehot = (expert_id[:, None] == jnp.arange(n_experts, dtype=jnp.int32)[None, :]).astype(jnp.int32)
    csum = jnp.cumsum(onehot, axis=0)
    counts = csum[-1]
    padded = (counts + bm - 1) // bm * bm
    pend = jnp.cumsum(padded)
    pstart = pend - padded
    dest = jnp.sum(onehot * (pstart[None, :] + csum - 1), axis=1)
    n_blocks = (n_assign + bm - 1) // bm + n_experts
    n_rows = n_blocks * bm
    row_asg = jnp.full((n_rows,), -1, jnp.int32).at[dest].set(jnp.arange(n_assign, dtype=jnp.int32))
    live = row_asg >= 0
    row_tok = jnp.where(live, row_asg // TOP_K, 0)
    row_w = jnp.where(live, wts[jnp.maximum(row_asg, 0)], 0.0)
    blk_start = jnp.arange(n_blocks, dtype=jnp.int32) * bm
    blk_expert = jnp.minimum(jnp.searchsorted(pend, blk_start, side='right'),
                             n_experts - 1).astype(jnp.int32)
    blk_first = jnp.concatenate([jnp.ones((1,), jnp.int32),
                                 (blk_expert[1:] != blk_expert[:-1]).astype(jnp.int32)])
    nxt_idx = jnp.searchsorted(blk_expert, blk_expert, side='right')
    blk_next = jnp.where(nxt_idx < n_blocks, blk_expert[jnp.minimum(nxt_idx, n_blocks - 1)],
                         -1).astype(jnp.int32)
    xb = x_bf[row_tok]
    yb = _moe_experts(xb, row_w[:, None], blk_expert, blk_first, blk_next, w_gate, w_up, w_down,
                      layer, bm=bm)
    pos = dest.reshape(n_tok, TOP_K)
    return yb[pos[:, 0]] + yb[pos[:, 1]]


def kernel(x, w_in, gdn_conv_w, gdn_a_log, gdn_dt_bias, gdn_norm_w, s5_lambda_re, s5_lambda_im, s5_b_re, s5_b_im, s5_c_re, s5_c_im, s5_d, s5_log_dt, s5_w_glu, w_out, ln1_g, ln1_b, router_group_w, router_group_b, router_expert_w, router_expert_b, expert_w_gate, expert_w_up, expert_w_down, ln2_g, ln2_b):
    bsz, seq, dm = x.shape
    assert bsz == 1
    depth = w_in.shape[0]
    heads = gdn_a_log.shape[1]
    gdn_width = heads * GDN_HEAD_DIM
    s5_width = s5_d.shape[1]
    c_z = 4 * gdn_width
    c_b = c_z + 2 * heads
    alpha = (2 * depth) ** 0.25
    n_experts = N_GROUPS * EXPERTS_PER_GROUP
    router_cols = LANES
    moe_bm = 256
    tail_cols = s5_width + LANES

    xf = x.reshape(seq, dm)
    xb = xf.astype(BF16)
    for i in range(depth):
        w_tail = jnp.concatenate(
            [w_in[i][:, c_b:], w_in[i][:, c_z:c_b], jnp.zeros((dm, LANES - 2 * heads), F32)], axis=1)
        proj = _matmul(xb, w_in, i, c_z, F32)
        proj_tail = _matmul(xb, w_tail[None], 0, tail_cols, F32)
        a_t = proj_tail[:, s5_width:s5_width + heads].T.reshape(heads, seq // GDN_CHUNK, GDN_CHUNK)
        b_t = proj_tail[:, s5_width + heads:s5_width + 2 * heads].T.reshape(
            heads, seq // GDN_CHUNK, GDN_CHUNK)
        y_gdn = _gdn(proj, gdn_conv_w[i], a_t, b_t, gdn_a_log[i], gdn_dt_bias[i], gdn_norm_w[i],
                     heads=heads)
        y_s5 = _s5(proj_tail, 0, s5_lambda_re[i], s5_lambda_im[i], s5_b_re[i], s5_b_im[i],
                   s5_c_re[i], s5_c_im[i], s5_d[i], s5_log_dt[i], s5_w_glu[i])
        mix = _matmul(jnp.concatenate([y_gdn, y_s5], axis=1), w_out, i, dm, F32)
        w_router = jnp.concatenate(
            [router_group_w[i],
             jnp.transpose(router_expert_w[i], (1, 0, 2)).reshape(dm, n_experts),
             jnp.zeros((dm, router_cols - N_GROUPS - n_experts), F32)], axis=1)
        xf, xb, logits = _residual_layer_norm(xf, mix, ln1_g[i], ln1_b[i], alpha, w_router)
        ffn = _moe(xb, logits, router_group_b[i], router_expert_b[i],
                   expert_w_gate, expert_w_up, expert_w_down, i, bm=moe_bm)
        xf, xb = _residual_layer_norm(xf, ffn, ln2_g[i], ln2_b[i], alpha)
    return xf.reshape(bsz, seq, dm)
```

```python
import functools
import math

import jax
import jax.numpy as jnp
from jax import lax
from jax.experimental import pallas as pl
from jax.experimental.pallas import tpu as pltpu

F32 = jnp.float32
BF16 = jnp.bfloat16

LANES = 128
SUBLANES = 8
VMEM_LIMIT_BYTES = 56 * 1024 * 1024

GDN_CHUNK = 64
GDN_HEAD_DIM = 128
SOLVE_CHUNKS = 2
CONV_WIDTH = 4
S5_GROUP = 16
S5_STATE = 64
S5_GROUPS_PER_SLAB = LANES // S5_GROUP
S5_SLAB_STATES = S5_GROUPS_PER_SLAB * S5_STATE
S5_TILES_PER_SLAB = S5_SLAB_STATES // LANES
N_GROUPS = 4
EXPERTS_PER_GROUP = 8
TOP_K = 2
LN_EPS = 1e-5
RMS_EPS = 1e-6
L2_EPS = 1e-6


def _params(sem):
    return pltpu.CompilerParams(dimension_semantics=sem, vmem_limit_bytes=VMEM_LIMIT_BYTES)


def _pick_tile(n, candidates):
    for c in candidates:
        if n % c == 0:
            return c
    return n


def _bdot(a, b):
    return jnp.dot(a.astype(BF16), b.astype(BF16), preferred_element_type=F32)


def _bdot_nt(a, b):
    return lax.dot_general(a.astype(BF16), b.astype(BF16), (((1,), (1,)), ((), ())),
                           preferred_element_type=F32)


def _bdot_tn(a, b):
    return lax.dot_general(a.astype(BF16), b.astype(BF16), (((0,), (0,)), ((), ())),
                           preferred_element_type=F32)


def _sigmoid(x):
    return 1.0 / (1.0 + jnp.exp(-x))


def _cast_rows_to_bf16(src_ref, dst_ref, rows_per_step):
    def body(r, carry):
        rows = pl.ds(pl.multiple_of(r * rows_per_step, rows_per_step), rows_per_step)
        dst_ref[rows, :] = src_ref[rows, :].astype(BF16)
        return carry

    lax.fori_loop(0, src_ref.shape[0] // rows_per_step, body, 0)


def _mm_kernel(a_ref, b_ref, o_ref, bb_ref, *, cast_rows, transposed):
    @pl.when(pl.program_id(1) == 0)
    def _():
        _cast_rows_to_bf16(b_ref, bb_ref, cast_rows)

    if transposed:
        acc = lax.dot_general(a_ref[...], bb_ref[...], (((1,), (1,)), ((), ())),
                              preferred_element_type=F32)
    else:
        acc = jnp.dot(a_ref[...], bb_ref[...], preferred_element_type=F32)
    o_ref[...] = acc.astype(o_ref.dtype)


def _matmul(a, w, layer, n_cols, out_dtype, *, transposed=False):
    m, kdim = a.shape
    tm = _pick_tile(m, (1024, 512, 256, 128))
    tn = _pick_tile(n_cols, (512, 384, 256, 128))
    if transposed:
        w_block, w_index, cast_rows = (None, tn, kdim), (lambda j, i: (layer, j, 0)), LANES
    else:
        w_block, w_index = (None, kdim, tn), (lambda j, i: (layer, 0, j))
        cast_rows = _pick_tile(kdim, (512, 256, 128))
    return pl.pallas_call(
        functools.partial(_mm_kernel, cast_rows=cast_rows, transposed=transposed),
        out_shape=jax.ShapeDtypeStruct((m, n_cols), out_dtype),
        grid=(n_cols // tn, m // tm),
        in_specs=[pl.BlockSpec((tm, kdim), lambda j, i: (i, 0)),
                  pl.BlockSpec(w_block, w_index)],
        out_specs=pl.BlockSpec((tm, tn), lambda j, i: (i, j)),
        scratch_shapes=[pltpu.VMEM(w_block[1:], BF16)],
        compiler_params=_params(("arbitrary", "arbitrary")),
        name="matmul",
    )(a, w)


def _gdn_kernel(q_ref, k_ref, v_ref, z_ref, wq_ref, wk_ref, wv_ref, a_ref, b_ref,
                alog_ref, dtb_ref, nw_ref, o_ref,
                s_ref, cq_ref, ck_ref, cv_ref, qs_ref, ks_ref, vs_ref, w_ref, qk_ref, dl_ref,
                *, tb, hb):
    c = GDN_CHUNK
    d = GDN_HEAD_DIM
    nc = tb // c

    @pl.when(pl.program_id(1) == 0)
    def _():
        s_ref[...] = jnp.zeros_like(s_ref)
        cq_ref[...] = jnp.zeros_like(cq_ref)
        ck_ref[...] = jnp.zeros_like(ck_ref)
        cv_ref[...] = jnp.zeros_like(cv_ref)

    def conv_body(ri, carry):
        r0 = pl.multiple_of(ri * c, c)
        rp = pl.multiple_of(jnp.maximum(r0 - SUBLANES, 0), SUBLANES)
        for x_ref, c_ref, w_ref, dst_ref, scale in ((q_ref, cq_ref, wq_ref, qs_ref, d ** -0.5),
                                                    (k_ref, ck_ref, wk_ref, ks_ref, 1.0),
                                                    (v_ref, cv_ref, wv_ref, vs_ref, None)):
            for h in range(hb):
                sl = slice(h * d, (h + 1) * d)
                prev = jnp.where(ri == 0, c_ref[:, sl], x_ref[pl.ds(rp, SUBLANES), sl])
                ext = jnp.concatenate([prev, x_ref[pl.ds(r0, c), sl]], axis=0)
                w = w_ref[:, sl]
                acc = ext[SUBLANES:, :] * w[CONV_WIDTH - 1:CONV_WIDTH, :]
                for j in range(1, CONV_WIDTH):
                    acc = acc + ext[SUBLANES - j:SUBLANES - j + c, :] * w[CONV_WIDTH - 1 - j:CONV_WIDTH - j, :]
                y = acc * _sigmoid(acc)
                if scale is not None:
                    y = y * (lax.rsqrt(jnp.sum(y * y, axis=-1, keepdims=True) + L2_EPS) * scale)
                dst_ref[pl.ds(r0, c), sl] = y
        return carry

    lax.fori_loop(0, nc, conv_body, 0)
    cq_ref[...] = q_ref[tb - SUBLANES:tb, :]
    ck_ref[...] = k_ref[tb - SUBLANES:tb, :]
    cv_ref[...] = v_ref[tb - SUBLANES:tb, :]

    ii = lax.broadcasted_iota(jnp.int32, (c, c), 0)
    jj = lax.broadcasted_iota(jnp.int32, (c, c), 1)
    causal = jj <= ii
    strict = jj < ii
    diag = jj == ii
    nw = nw_ref[...]

    eye = jnp.where(diag, 1.0, 0.0).astype(F32)
    heads = range(hb)

    sls = [slice(h * d, (h + 1) * d) for h in heads]

    def solve_body(pi, carry):
        items = [(pi * SOLVE_CHUNKS + cj, h) for cj in range(SOLVE_CHUNKS) for h in heads]
        n = range(len(items))
        rows = [pl.ds(pl.multiple_of(ci * c, c), c) for ci, _ in items]
        qc = [qs_ref[rows[i], sls[h]] for i, (_, h) in enumerate(items)]
        kc = [ks_ref[rows[i], sls[h]] for i, (_, h) in enumerate(items)]
        vc = [vs_ref[rows[i], sls[h]] for i, (_, h) in enumerate(items)]
        decay_col, beta_col, gamma = [], [], []
        for ci, h in items:
            a_row = a_ref[h, pl.ds(ci, 1), :]
            b_row = b_ref[h, pl.ds(ci, 1), :]
            sp_in = a_row + dtb_ref[h]
            softplus = jnp.maximum(sp_in, 0.0) + jnp.log1p(jnp.exp(-jnp.abs(sp_in)))
            g_row = -jnp.exp(alog_ref[h]) * softplus
            beta_row = _sigmoid(b_row)
            g_b = jnp.broadcast_to(g_row, (c, c))
            dcol = jnp.sum(jnp.where(causal, g_b, 0.0), axis=1, keepdims=True)
            drow = jnp.sum(jnp.where(diag, jnp.broadcast_to(dcol, (c, c)), 0.0),
                           axis=0, keepdims=True)
            bcol = jnp.sum(jnp.where(diag, jnp.broadcast_to(beta_row, (c, c)), 0.0),
                           axis=1, keepdims=True)
            decay_col.append(dcol)
            beta_col.append(bcol)
            gamma.append(jnp.where(causal, jnp.exp(jnp.where(causal, dcol - drow, 0.0)), 0.0))
        k_beta = [kc[i] * beta_col[i] for i in n]
        kq = [_bdot_nt(jnp.concatenate([k_beta[i], qc[i]], axis=0), kc[i]) for i in n]
        pw = [jnp.where(strict, -(kq[i][:c, :] * gamma[i]), 0.0) for i in n]
        p = [eye + pw[i] for i in n]
        pw = [_bdot(pw[i], pw[i]) for i in n]
        span = 2
        while 2 * span < c:
            st = [_bdot(jnp.concatenate([pw[i], p[i]], axis=0), pw[i]) for i in n]
            p = [p[i] + st[i][c:, :] for i in n]
            pw = [st[i][:c, :] for i in n]
            span *= 2
        t_inv = [p[i] + _bdot(p[i], pw[i]) for i in n]
        e_col = [jnp.exp(decay_col[i]) for i in n]
        uw = [_bdot(t_inv[i], jnp.concatenate([vc[i] * beta_col[i], k_beta[i] * e_col[i]], axis=1))
              for i in n]
        for i, (ci, h) in enumerate(items):
            decay_last = decay_col[i][c - 1:c, :]
            vs_ref[rows[i], sls[h]] = uw[i][:, :d]
            w_ref[rows[i], sls[h]] = uw[i][:, d:]
            qs_ref[rows[i], sls[h]] = qc[i] * e_col[i]
            ks_ref[rows[i], sls[h]] = kc[i] * jnp.exp(decay_last - decay_col[i])
            qk_ref[h, rows[i], :] = kq[i][c:, :] * gamma[i]
            dl_ref[ci * hb + h] = jnp.broadcast_to(jnp.exp(decay_last), (SUBLANES, d))
        return carry

    def state_body(ci, carry):
        rows = pl.ds(pl.multiple_of(ci * c, c), c)
        s = [s_ref[h] for h in heads]
        ws = [_bdot(jnp.concatenate([w_ref[rows, sls[h]], qs_ref[rows, sls[h]]], axis=0), s[h])
              for h in heads]
        v_new = [vs_ref[rows, sls[h]] - ws[h][:c, :] for h in heads]
        o_intra = [_bdot(qk_ref[h, rows, :], v_new[h]) for h in heads]
        kv = [_bdot_tn(ks_ref[rows, sls[h]], v_new[h]) for h in heads]
        for h in heads:
            s_ref[h] = s[h] * dl_ref[ci * hb + h][0:1, :] + kv[h]
            o = ws[h][c:, :] + o_intra[h]
            o = o * lax.rsqrt(jnp.mean(o * o, axis=-1, keepdims=True) + RMS_EPS) * nw
            zc = z_ref[rows, sls[h]]
            o = o * (zc * _sigmoid(zc))
            o_ref[rows, sls[h]] = o.astype(o_ref.dtype)
        return carry

    lax.fori_loop(0, nc // SOLVE_CHUNKS, solve_body, 0)
    lax.fori_loop(0, nc, state_body, 0)


def _gdn(proj, conv_w, a_t, b_t, a_log, dt_bias, norm_w, *, heads):
    seq = proj.shape[0]
    d = GDN_HEAD_DIM
    hb = _pick_tile(heads, (8, 6, 4, 2))
    tb = _pick_tile(seq, (512,))
    n_hb = heads // hb
    wblk = hb * d
    kernel = functools.partial(_gdn_kernel, tb=tb, hb=hb)

    def col(offset):
        return lambda h, t: (t, offset * n_hb + h)

    def wcol(offset):
        return lambda h, t: (0, offset * n_hb + h)

    return pl.pallas_call(
        kernel,
        out_shape=jax.ShapeDtypeStruct((seq, heads * d), BF16),
        grid=(n_hb, seq // tb),
        in_specs=[pl.BlockSpec((tb, wblk), col(0)),
                  pl.BlockSpec((tb, wblk), col(1)),
                  pl.BlockSpec((tb, wblk), col(2)),
                  pl.BlockSpec((tb, wblk), col(3)),
                  pl.BlockSpec((CONV_WIDTH, wblk), wcol(0)),
                  pl.BlockSpec((CONV_WIDTH, wblk), wcol(1)),
                  pl.BlockSpec((CONV_WIDTH, wblk), wcol(2)),
                  pl.BlockSpec((hb, tb // GDN_CHUNK, GDN_CHUNK), lambda h, t: (h, t, 0)),
                  pl.BlockSpec((hb, tb // GDN_CHUNK, GDN_CHUNK), lambda h, t: (h, t, 0)),
                  pl.BlockSpec((hb, 1, 1), lambda h, t: (h, 0, 0)),
                  pl.BlockSpec((hb, 1, 1), lambda h, t: (h, 0, 0)),
                  pl.BlockSpec((1, d), lambda h, t: (0, 0))],
        out_specs=pl.BlockSpec((tb, wblk), lambda h, t: (t, h)),
        scratch_shapes=[pltpu.VMEM((hb, d, d), F32),
                        pltpu.VMEM((SUBLANES, wblk), F32),
                        pltpu.VMEM((SUBLANES, wblk), F32),
                        pltpu.VMEM((SUBLANES, wblk), F32),
                        pltpu.VMEM((tb, wblk), F32),
                        pltpu.VMEM((tb, wblk), F32),
                        pltpu.VMEM((tb, wblk), F32),
                        pltpu.VMEM((tb, wblk), F32),
                        pltpu.VMEM((hb, tb, GDN_CHUNK), F32),
                        pltpu.VMEM((tb // GDN_CHUNK * hb, SUBLANES, d), F32)],
        compiler_params=_params(("parallel", "arbitrary")),
        name="gdn",
    )(proj, proj, proj, proj, conv_w, conv_w, conv_w, a_t, b_t,
      a_log.reshape(heads, 1, 1), dt_bias.reshape(heads, 1, 1), norm_w.reshape(1, d))


def _gelu_tanh(y):
    return 0.5 * y * (1.0 + jnp.tanh(math.sqrt(2.0 / math.pi) * (y + 0.044715 * (y * y * y))))


def _s5_kernel(u_ref, bre_ref, bim_ref, cre_ref, cim_ref, are_ref, aim_ref, d_ref, wglu_ref,
               o_ref, xr_ref, xi_ref, str_ref, sti_ref, y_ref, wglu_bf_ref, *, tb, pitch, slabs):
    tiles = slabs * S5_TILES_PER_SLAB
    n_vreg = tiles // SUBLANES

    @pl.when(pl.program_id(0) == 0)
    def _():
        str_ref[...] = jnp.zeros_like(str_ref)
        sti_ref[...] = jnp.zeros_like(sti_ref)
        _cast_rows_to_bf16(wglu_ref, wglu_bf_ref, LANES)

    for s in range(slabs):
        ub = u_ref[:, s * LANES:(s + 1) * LANES].astype(BF16)
        r = jnp.dot(ub, bre_ref[s], preferred_element_type=F32)
        m = jnp.dot(ub, bim_ref[s], preferred_element_type=F32)
        for t4 in range(S5_TILES_PER_SLAB):
            lt = s * S5_TILES_PER_SLAB + t4
            xr_ref[lt * pitch:lt * pitch + tb, :] = r[:, t4 * LANES:(t4 + 1) * LANES]
            xi_ref[lt * pitch:lt * pitch + tb, :] = m[:, t4 * LANES:(t4 + 1) * LANES]

    ar = [are_ref[j] for j in range(n_vreg)]
    ai = [aim_ref[j] for j in range(n_vreg)]

    def step(t, carry):
        xr, xi = carry
        nxr, nxi = [], []
        for j in range(n_vreg):
            rows = pl.ds(j * SUBLANES * pitch + t, SUBLANES, stride=pitch)
            br = xr_ref[rows, :]
            bi = xi_ref[rows, :]
            r = ar[j] * xr[j] - ai[j] * xi[j] + br
            m = ar[j] * xi[j] + ai[j] * xr[j] + bi
            xr_ref[rows, :] = r
            xi_ref[rows, :] = m
            nxr.append(r)
            nxi.append(m)
        return tuple(nxr), tuple(nxi)

    x0 = (tuple(str_ref[j] for j in range(n_vreg)), tuple(sti_ref[j] for j in range(n_vreg)))
    xr_f, xi_f = lax.fori_loop(0, tb, step, x0)
    for j in range(n_vreg):
        str_ref[j] = xr_f[j]
        sti_ref[j] = xi_f[j]

    for s in range(slabs):
        acc = jnp.zeros((tb, LANES), F32)
        for t4 in range(S5_TILES_PER_SLAB):
            lt = s * S5_TILES_PER_SLAB + t4
            xr = xr_ref[lt * pitch:lt * pitch + tb, :].astype(BF16)
            xi = xi_ref[lt * pitch:lt * pitch + tb, :].astype(BF16)
            acc = acc + jnp.dot(xr, cre_ref[s, t4 * LANES:(t4 + 1) * LANES, :],
                                preferred_element_type=F32)
            acc = acc - jnp.dot(xi, cim_ref[s, t4 * LANES:(t4 + 1) * LANES, :],
                                preferred_element_type=F32)
        sl = slice(s * LANES, (s + 1) * LANES)
        y_ref[:, sl] = _gelu_tanh(acc + d_ref[:, sl] * u_ref[:, sl])

    y = y_ref[...]
    gate = jnp.dot(y.astype(BF16), wglu_bf_ref[...], preferred_element_type=F32)
    o_ref[...] = (y * _sigmoid(gate)).astype(o_ref.dtype)


def _s5(proj, u_col_block, lam_re, lam_im, b_re, b_im, c_re, c_im, d_skip, log_dt, w_glu):
    seq = proj.shape[0]
    groups = lam_re.shape[0]
    width = groups * S5_GROUP
    slabs = width // LANES
    tiles = slabs * S5_TILES_PER_SLAB
    assert tiles % SUBLANES == 0
    tb = _pick_tile(seq, (512,))
    pitch = tb + SUBLANES

    dt = jnp.exp(log_dt)[:, None]
    mag = jnp.exp(lam_re * dt)
    ab_re, ab_im = mag * jnp.cos(lam_im * dt), mag * jnp.sin(lam_im * dt)
    den = lam_re * lam_re + lam_im * lam_im
    nr, ni = ab_re - 1.0, ab_im
    coef_re = (nr * lam_re + ni * lam_im) / den
    coef_im = (ni * lam_re - nr * lam_im) / den
    bb_re = coef_re[..., None] * b_re - coef_im[..., None] * b_im
    bb_im = coef_re[..., None] * b_im + coef_im[..., None] * b_re
    eye = jnp.eye(S5_GROUPS_PER_SLAB, dtype=F32)

    def b_blockdiag(bb):
        t = bb.reshape(slabs, S5_GROUPS_PER_SLAB, S5_STATE, S5_GROUP)
        m = jnp.einsum('saph,ab->sahbp', t, eye)
        return m.reshape(slabs, LANES, S5_SLAB_STATES).astype(BF16)

    def c_blockdiag(cc):
        t = cc.reshape(slabs, S5_GROUPS_PER_SLAB, S5_GROUP, S5_STATE)
        m = jnp.einsum('sahp,ab->sapbh', t, eye)
        return m.reshape(slabs, S5_SLAB_STATES, LANES).astype(BF16)

    n_vreg = tiles // SUBLANES
    kernel = functools.partial(_s5_kernel, tb=tb, pitch=pitch, slabs=slabs)
    full3 = lambda i: (0, 0, 0)
    return pl.pallas_call(
        kernel,
        out_shape=jax.ShapeDtypeStruct((seq, width), BF16),
        grid=(seq // tb,),
        in_specs=[pl.BlockSpec((tb, width), lambda i: (i, u_col_block)),
                  pl.BlockSpec((slabs, LANES, S5_SLAB_STATES), full3),
                  pl.BlockSpec((slabs, LANES, S5_SLAB_STATES), full3),
                  pl.BlockSpec((slabs, S5_SLAB_STATES, LANES), full3),
                  pl.BlockSpec((slabs, S5_SLAB_STATES, LANES), full3),
                  pl.BlockSpec((n_vreg, SUBLANES, LANES), full3),
                  pl.BlockSpec((n_vreg, SUBLANES, LANES), full3),
                  pl.BlockSpec((1, width), lambda i: (0, 0)),
                  pl.BlockSpec((width, width), lambda i: (0, 0))],
        out_specs=pl.BlockSpec((tb, width), lambda i: (i, 0)),
        scratch_shapes=[pltpu.VMEM((tiles * pitch, LANES), F32),
                        pltpu.VMEM((tiles * pitch, LANES), F32),
                        pltpu.VMEM((n_vreg, SUBLANES, LANES), F32),
                        pltpu.VMEM((n_vreg, SUBLANES, LANES), F32),
                        pltpu.VMEM((tb, width), F32),
                        pltpu.VMEM((width, width), BF16)],
        compiler_params=_params(("arbitrary",)),
        name="s5",
    )(proj, b_blockdiag(bb_re), b_blockdiag(bb_im), c_blockdiag(c_re), c_blockdiag(c_im),
      ab_re.reshape(n_vreg, SUBLANES, LANES), ab_im.reshape(n_vreg, SUBLANES, LANES),
      d_skip.reshape(1, width), w_glu)


def _ln_rows(h, g_ref, b_ref):
    mu = jnp.mean(h, axis=-1, keepdims=True)
    hc = h - mu
    var = jnp.mean(hc * hc, axis=-1, keepdims=True)
    return hc * lax.rsqrt(var + LN_EPS) * g_ref[...] + b_ref[...]


def _ln_router_kernel(x_ref, m_ref, g_ref, b_ref, wr_ref, o_ref, lg_ref, *, alpha):
    y = _ln_rows(alpha * x_ref[...] + m_ref[...], g_ref, b_ref)
    o_ref[...] = y
    lg_ref[...] = jnp.dot(y, wr_ref[...], preferred_element_type=F32,
                          precision=lax.Precision.HIGHEST)


def _ln_pair_kernel(x_ref, m0_ref, m1_ref, g_ref, b_ref, o_ref, ob_ref, *, alpha):
    y = _ln_rows(alpha * x_ref[...] + (m0_ref[...] + m1_ref[...]), g_ref, b_ref)
    o_ref[...] = y
    ob_ref[...] = y.astype(BF16)


def _residual_layer_norm(x, addends, g, b, alpha, w_router=None):
    seq, dm = x.shape
    tm = _pick_tile(seq, (256, 128))
    row = pl.BlockSpec((tm, dm), lambda i: (i, 0))
    vec = pl.BlockSpec((1, dm), lambda i: (0, 0))
    args = [x, *addends, g.reshape(1, dm), b.reshape(1, dm)]
    in_specs = [row] * (1 + len(addends)) + [vec, vec]
    if w_router is None:
        assert len(addends) == 2
        kernel = functools.partial(_ln_pair_kernel, alpha=alpha)
        out_shape = [jax.ShapeDtypeStruct((seq, dm), F32), jax.ShapeDtypeStruct((seq, dm), BF16)]
        out_specs = [row, row]
    else:
        assert len(addends) == 1
        kernel = functools.partial(_ln_router_kernel, alpha=alpha)
        nr = w_router.shape[1]
        args.append(w_router)
        in_specs.append(pl.BlockSpec((dm, nr), lambda i: (0, 0)))
        out_shape = [jax.ShapeDtypeStruct((seq, dm), F32), jax.ShapeDtypeStruct((seq, nr), F32)]
        out_specs = [row, pl.BlockSpec((tm, nr), lambda i: (i, 0))]
    return pl.pallas_call(
        kernel,
        out_shape=out_shape,
        grid=(seq // tm,),
        in_specs=in_specs,
        out_specs=out_specs,
        compiler_params=_params(("parallel",)),
        name="residual_layer_norm",
    )(*args)


def _moe_kernel(be_ref, first_ref, nxt_ref, nused_ref, x_ref, rw_ref, wg_hbm, wu_hbm, wd_hbm, o_ref,
                sg_ref, su_ref, sd_ref, bg_ref, bu_ref, bd_ref, sem, *, layer):
    b = pl.program_id(0)

    def weight_copies(e):
        return (pltpu.make_async_copy(wg_hbm.at[layer, e], sg_ref, sem.at[0]),
                pltpu.make_async_copy(wu_hbm.at[layer, e], su_ref, sem.at[1]),
                pltpu.make_async_copy(wd_hbm.at[layer, e], sd_ref, sem.at[2]))

    @pl.when(b == 0)
    def _():
        for cp in weight_copies(be_ref[0]):
            cp.start()

    @pl.when(first_ref[b] == 1)
    def _():
        for cp in weight_copies(be_ref[b]):
            cp.wait()
        _cast_rows_to_bf16(sg_ref, bg_ref, 512)
        _cast_rows_to_bf16(su_ref, bu_ref, 512)
        _cast_rows_to_bf16(sd_ref, bd_ref, 64)

        @pl.when(nxt_ref[b] >= 0)
        def _():
            for cp in weight_copies(nxt_ref[b]):
                cp.start()

    @pl.when(b >= nused_ref[0])
    def _():
        o_ref[...] = jnp.zeros_like(o_ref)

    @pl.when(b < nused_ref[0])
    def _():
        x = x_ref[...].astype(BF16)
        g = jnp.dot(x, bg_ref[...], preferred_element_type=F32)
        u = jnp.dot(x, bu_ref[...], preferred_element_type=F32)
        hid = (g * _sigmoid(g)) * u
        y = jnp.dot(hid.astype(BF16), bd_ref[...], preferred_element_type=F32)
        o_ref[...] = y * rw_ref[...]


def _moe_experts(xb, row_w, blk_expert, blk_first, blk_next, n_used, w_gate, w_up, w_down, layer,
                 *, bm):
    n_rows, dm = xb.shape
    de = w_gate.shape[3]
    n_blocks = n_rows // bm
    assert dm % 512 == 0 and de % 64 == 0
    grid_spec = pltpu.PrefetchScalarGridSpec(
        num_scalar_prefetch=4,
        grid=(n_blocks,),
        in_specs=[pl.BlockSpec((bm, dm), lambda i, be, fi, nx, nu: (i, 0)),
                  pl.BlockSpec((bm, 1), lambda i, be, fi, nx, nu: (i, 0)),
                  pl.BlockSpec(memory_space=pl.ANY),
                  pl.BlockSpec(memory_space=pl.ANY),
                  pl.BlockSpec(memory_space=pl.ANY)],
        out_specs=pl.BlockSpec((bm, dm), lambda i, be, fi, nx, nu: (i, 0)),
        scratch_shapes=[pltpu.VMEM((dm, de), F32), pltpu.VMEM((dm, de), F32),
                        pltpu.VMEM((de, dm), F32),
                        pltpu.VMEM((dm, de), BF16), pltpu.VMEM((dm, de), BF16),
                        pltpu.VMEM((de, dm), BF16),
                        pltpu.SemaphoreType.DMA((3,))],
    )
    return pl.pallas_call(
        functools.partial(_moe_kernel, layer=layer),
        out_shape=jax.ShapeDtypeStruct((n_rows, dm), F32),
        grid_spec=grid_spec,
        compiler_params=_params(("arbitrary",)),
        name="moe_experts",
    )(blk_expert, blk_first, blk_next, n_used, xb, row_w, w_gate, w_up, w_down)


def _moe(x, logits, b_rg, b_re, w_gate, w_up, w_down, layer, *, bm):
    n_tok, dm = x.shape
    n_experts = N_GROUPS * EXPERTS_PER_GROUP
    grp_logits = logits[:, :N_GROUPS] + b_rg
    exp_logits = logits[:, N_GROUPS:N_GROUPS + n_experts].reshape(n_tok, N_GROUPS, EXPERTS_PER_GROUP) + b_re
    grp_probs = jax.nn.softmax(grp_logits, axis=-1)
    grp_idx = jnp.argmax(grp_probs, axis=-1).astype(jnp.int32)[:, None]
    grp_p = jnp.max(grp_probs, axis=-1, keepdims=True)
    sel = exp_logits[:, 0]
    for g in range(1, N_GROUPS):
        sel = jnp.where(grp_idx == g, exp_logits[:, g], sel)
    lane = jnp.arange(EXPERTS_PER_GROUP, dtype=jnp.int32)[None, :]
    idx1 = jnp.argmax(sel, axis=-1).astype(jnp.int32)[:, None]
    rest = jnp.where(lane == idx1, -jnp.inf, sel)
    idx2 = jnp.argmax(rest, axis=-1).astype(jnp.int32)[:, None]
    top_idx = jnp.concatenate([idx1, idx2], axis=1)
    top_logit = jnp.concatenate([jnp.max(sel, axis=-1, keepdims=True),
                                 jnp.max(rest, axis=-1, keepdims=True)], axis=1)
    top_w = jax.nn.softmax(top_logit, axis=-1) * grp_p
    expert_id = (grp_idx * EXPERTS_PER_GROUP + top_idx).reshape(-1).astype(jnp.int32)
    n_assign = n_tok * TOP_K
    wts = top_w.reshape(-1)
    onehot = (expert_id[:, None] == jnp.arange(n_experts, dtype=jnp.int32)[None, :]).astype(jnp.int32)
    csum = jnp.cumsum(onehot, axis=0)
    counts = csum[-1]
    padded = (counts + bm - 1) // bm * bm
    pend = jnp.cumsum(padded)
    pstart = pend - padded
    dest = jnp.sum(onehot * (pstart[None, :] + csum - 1), axis=1)
    n_blocks = (n_assign + bm - 1) // bm + n_experts
    n_rows = n_blocks * bm
    row_asg = jnp.full((n_rows,), -1, jnp.int32).at[dest].set(jnp.arange(n_assign, dtype=jnp.int32))
    live = row_asg >= 0
    row_tok = jnp.where(live, row_asg // TOP_K, 0)
    row_w = jnp.where(live, wts[jnp.maximum(row_asg, 0)], 0.0)
    blk_start = jnp.arange(n_blocks, dtype=jnp.int32) * bm
    blk_expert = jnp.minimum(jnp.searchsorted(pend, blk_start, side='right'),
                             n_experts - 1).astype(jnp.int32)
    blk_first = jnp.concatenate([jnp.ones((1,), jnp.int32),
                                 (blk_expert[1:] != blk_expert[:-1]).astype(jnp.int32)])
    nxt_idx = jnp.searchsorted(blk_expert, blk_expert, side='right')
    blk_next = jnp.where(nxt_idx < n_blocks, blk_expert[jnp.minimum(nxt_idx, n_blocks - 1)],
                         -1).astype(jnp.int32)
    xg = x[row_tok]
    n_used = (pend[-1:] // bm).astype(jnp.int32)
    yb = _moe_experts(xg, row_w[:, None], blk_expert, blk_first, blk_next, n_used,
                      w_gate, w_up, w_down, layer, bm=bm)
    pos = dest.reshape(n_tok, TOP_K)
    return yb[pos[:, 0]], yb[pos[:, 1]]


def kernel(x, w_in, gdn_conv_w, gdn_a_log, gdn_dt_bias, gdn_norm_w, s5_lambda_re, s5_lambda_im, s5_b_re, s5_b_im, s5_c_re, s5_c_im, s5_d, s5_log_dt, s5_w_glu, w_out, ln1_g, ln1_b, router_group_w, router_group_b, router_expert_w, router_expert_b, expert_w_gate, expert_w_up, expert_w_down, ln2_g, ln2_b):
    bsz, seq, dm = x.shape
    assert bsz == 1
    depth = w_in.shape[0]
    heads = gdn_a_log.shape[1]
    gdn_width = heads * GDN_HEAD_DIM
    s5_width = s5_d.shape[1]
    c_z = 4 * gdn_width
    c_b = c_z + 2 * heads
    alpha = (2 * depth) ** 0.25
    n_experts = N_GROUPS * EXPERTS_PER_GROUP
    router_cols = LANES
    moe_bm = 256
    tail_cols = s5_width + LANES

    w_in_t = jnp.swapaxes(w_in, 1, 2)
    xf = x.reshape(seq, dm)
    xb = xf.astype(BF16)
    for i in range(depth):
        w_rest = lax.slice(w_in_t, (i, c_z, 0), (i + 1, w_in_t.shape[1], dm))[0]
        w_tail = jnp.concatenate(
            [w_rest[2 * heads:], w_rest[:2 * heads], jnp.zeros((LANES - 2 * heads, dm), F32)], axis=0)
        proj = _matmul(xb, w_in_t, i, c_z, F32, transposed=True)
        proj_tail = _matmul(xb, w_tail[None], 0, tail_cols, F32, transposed=True)
        a_t = proj_tail[:, s5_width:s5_width + heads].T.reshape(heads, seq // GDN_CHUNK, GDN_CHUNK)
        b_t = proj_tail[:, s5_width + heads:s5_width + 2 * heads].T.reshape(
            heads, seq // GDN_CHUNK, GDN_CHUNK)
        y_gdn = _gdn(proj, gdn_conv_w[i], a_t, b_t, gdn_a_log[i], gdn_dt_bias[i], gdn_norm_w[i],
                     heads=heads)
        y_s5 = _s5(proj_tail, 0, s5_lambda_re[i], s5_lambda_im[i], s5_b_re[i], s5_b_im[i],
                   s5_c_re[i], s5_c_im[i], s5_d[i], s5_log_dt[i], s5_w_glu[i])
        mix = _matmul(jnp.concatenate([y_gdn, y_s5], axis=1), w_out, i, dm, F32)
        w_router = jnp.concatenate(
            [router_group_w[i],
             jnp.transpose(router_expert_w[i], (1, 0, 2)).reshape(dm, n_experts),
             jnp.zeros((dm, router_cols - N_GROUPS - n_experts), F32)], axis=1)
        xf, logits = _residual_layer_norm(xf, (mix,), ln1_g[i], ln1_b[i], alpha, w_router)
        ffn_pair = _moe(xf, logits, router_group_b[i], router_expert_b[i],
                        expert_w_gate, expert_w_up, expert_w_down, i, bm=moe_bm)
        xf, xb = _residual_layer_norm(xf, ffn_pair, ln2_g[i], ln2_b[i], alpha)
    return xf.reshape(bsz, seq, dm)
```

```python
import functools
import math

import jax
import jax.numpy as jnp
from jax import lax
from jax.experimental import pallas as pl
from jax.experimental.pallas import tpu as pltpu

F32 = jnp.float32
BF16 = jnp.bfloat16

LANES = 128
SUBLANES = 8
VMEM_LIMIT_BYTES = 56 * 1024 * 1024

GDN_CHUNK = 64
GDN_HEAD_DIM = 128
SOLVE_CHUNKS = 2
CONV_WIDTH = 4
S5_GROUP = 16
S5_STATE = 64
S5_GROUPS_PER_SLAB = LANES // S5_GROUP
S5_SLAB_STATES = S5_GROUPS_PER_SLAB * S5_STATE
S5_TILES_PER_SLAB = S5_SLAB_STATES // LANES
N_GROUPS = 4
EXPERTS_PER_GROUP = 8
TOP_K = 2
LN_EPS = 1e-5
RMS_EPS = 1e-6
L2_EPS = 1e-6


def _params(sem):
    return pltpu.CompilerParams(dimension_semantics=sem, vmem_limit_bytes=VMEM_LIMIT_BYTES)


def _pick_tile(n, candidates):
    for c in candidates:
        if n % c == 0:
            return c
    return n


def _bdot(a, b):
    return jnp.dot(a.astype(BF16), b.astype(BF16), preferred_element_type=F32)


def _bdot_nt(a, b):
    return lax.dot_general(a.astype(BF16), b.astype(BF16), (((1,), (1,)), ((), ())),
                           preferred_element_type=F32)


def _bdot_tn(a, b):
    return lax.dot_general(a.astype(BF16), b.astype(BF16), (((0,), (0,)), ((), ())),
                           preferred_element_type=F32)


def _sigmoid(x):
    return 1.0 / (1.0 + jnp.exp(-x))


def _cast_rows_to_bf16(src_ref, dst_ref, rows_per_step):
    def body(r, carry):
        rows = pl.ds(pl.multiple_of(r * rows_per_step, rows_per_step), rows_per_step)
        dst_ref[rows, :] = src_ref[rows, :].astype(BF16)
        return carry

    lax.fori_loop(0, src_ref.shape[0] // rows_per_step, body, 0)


def _mm_kernel(a_ref, b_ref, o_ref, bb_ref, *, cast_rows, transposed):
    @pl.when(pl.program_id(1) == 0)
    def _():
        _cast_rows_to_bf16(b_ref, bb_ref, cast_rows)

    if transposed:
        acc = lax.dot_general(a_ref[...], bb_ref[...], (((1,), (1,)), ((), ())),
                              preferred_element_type=F32)
    else:
        acc = jnp.dot(a_ref[...], bb_ref[...], preferred_element_type=F32)
    o_ref[...] = acc.astype(o_ref.dtype)


def _matmul(a, w, layer, n_cols, out_dtype, *, transposed=False):
    m, kdim = a.shape
    tm = _pick_tile(m, (1024, 512, 256, 128))
    tn = _pick_tile(n_cols, (512, 384, 256, 128))
    if transposed:
        w_block, w_index, cast_rows = (None, tn, kdim), (lambda j, i: (layer, j, 0)), LANES
    else:
        w_block, w_index = (None, kdim, tn), (lambda j, i: (layer, 0, j))
        cast_rows = _pick_tile(kdim, (512, 256, 128))
    return pl.pallas_call(
        functools.partial(_mm_kernel, cast_rows=cast_rows, transposed=transposed),
        out_shape=jax.ShapeDtypeStruct((m, n_cols), out_dtype),
        grid=(n_cols // tn, m // tm),
        in_specs=[pl.BlockSpec((tm, kdim), lambda j, i: (i, 0)),
                  pl.BlockSpec(w_block, w_index)],
        out_specs=pl.BlockSpec((tm, tn), lambda j, i: (i, j)),
        scratch_shapes=[pltpu.VMEM(w_block[1:], BF16)],
        compiler_params=_params(("arbitrary", "arbitrary")),
        name="matmul",
    )(a, w)


def _gdn_kernel(q_ref, k_ref, v_ref, z_ref, wq_ref, wk_ref, wv_ref, a_ref, b_ref,
                alog_ref, dtb_ref, nw_ref, o_ref,
                s_ref, cq_ref, ck_ref, cv_ref, qs_ref, ks_ref, vs_ref, w_ref, qk_ref, dl_ref, ext_refs,
                *, tb, hb):
    c = GDN_CHUNK
    d = GDN_HEAD_DIM
    nc = tb // c

    @pl.when(pl.program_id(1) == 0)
    def _():
        s_ref[...] = jnp.zeros_like(s_ref)
        cq_ref[...] = jnp.zeros_like(cq_ref)
        ck_ref[...] = jnp.zeros_like(ck_ref)
        cv_ref[...] = jnp.zeros_like(cv_ref)

    def conv_body(ri, carry):
        r0 = pl.multiple_of(ri * c, c)
        rp = pl.multiple_of(jnp.maximum(r0 - SUBLANES, 0), SUBLANES)
        for a, (x_ref, c_ref, w_ref, dst_ref, scale) in enumerate(
                ((q_ref, cq_ref, wq_ref, qs_ref, d ** -0.5),
                 (k_ref, ck_ref, wk_ref, ks_ref, 1.0),
                 (v_ref, cv_ref, wv_ref, vs_ref, None))):
            for h in range(hb):
                sl = slice(h * d, (h + 1) * d)
                ext_ref = ext_refs.at[a * hb + h]
                ext_ref[0:SUBLANES, :] = jnp.where(ri == 0, c_ref[:, sl],
                                                   x_ref[pl.ds(rp, SUBLANES), sl])
                cur = x_ref[pl.ds(r0, c), sl]
                ext_ref[SUBLANES:, :] = cur
                w = w_ref[:, sl]
                acc = cur * w[CONV_WIDTH - 1:CONV_WIDTH, :]
                for j in range(1, CONV_WIDTH):
                    acc = acc + (ext_ref[SUBLANES - j:SUBLANES - j + c, :]
                                 * w[CONV_WIDTH - 1 - j:CONV_WIDTH - j, :])
                y = acc * _sigmoid(acc)
                if scale is not None:
                    y = y * (lax.rsqrt(jnp.sum(y * y, axis=-1, keepdims=True) + L2_EPS) * scale)
                dst_ref[pl.ds(r0, c), sl] = y
        return carry

    lax.fori_loop(0, nc, conv_body, 0)
    cq_ref[...] = q_ref[tb - SUBLANES:tb, :]
    ck_ref[...] = k_ref[tb - SUBLANES:tb, :]
    cv_ref[...] = v_ref[tb - SUBLANES:tb, :]

    ii = lax.broadcasted_iota(jnp.int32, (c, c), 0)
    jj = lax.broadcasted_iota(jnp.int32, (c, c), 1)
    causal = jj <= ii
    strict = jj < ii
    diag = jj == ii
    nw = nw_ref[...]

    eye = jnp.where(diag, 1.0, 0.0).astype(F32)
    heads = range(hb)

    sls = [slice(h * d, (h + 1) * d) for h in heads]

    def solve_body(pi, carry):
        items = [(pi * SOLVE_CHUNKS + cj, h) for cj in range(SOLVE_CHUNKS) for h in heads]
        n = range(len(items))
        rows = [pl.ds(pl.multiple_of(ci * c, c), c) for ci, _ in items]
        qc = [qs_ref[rows[i], sls[h]] for i, (_, h) in enumerate(items)]
        kc = [ks_ref[rows[i], sls[h]] for i, (_, h) in enumerate(items)]
        vc = [vs_ref[rows[i], sls[h]] for i, (_, h) in enumerate(items)]
        decay_col, beta_col, gamma = [], [], []
        for ci, h in items:
            a_row = a_ref[h, pl.ds(ci, 1), :]
            b_row = b_ref[h, pl.ds(ci, 1), :]
            sp_in = a_row + dtb_ref[h]
            softplus = jnp.maximum(sp_in, 0.0) + jnp.log1p(jnp.exp(-jnp.abs(sp_in)))
            g_row = -jnp.exp(alog_ref[h]) * softplus
            beta_row = _sigmoid(b_row)
            g_b = jnp.broadcast_to(g_row, (c, c))
            dcol = jnp.sum(jnp.where(causal, g_b, 0.0), axis=1, keepdims=True)
            drow = jnp.sum(jnp.where(diag, jnp.broadcast_to(dcol, (c, c)), 0.0),
                           axis=0, keepdims=True)
            bcol = jnp.sum(jnp.where(diag, jnp.broadcast_to(beta_row, (c, c)), 0.0),
                           axis=1, keepdims=True)
            decay_col.append(dcol)
            beta_col.append(bcol)
            gamma.append(jnp.where(causal, jnp.exp(jnp.where(causal, dcol - drow, 0.0)), 0.0))
        k_beta = [kc[i] * beta_col[i] for i in n]
        kq = [_bdot_nt(jnp.concatenate([k_beta[i], qc[i]], axis=0), kc[i]) for i in n]
        pw = [jnp.where(strict, -(kq[i][:c, :] * gamma[i]), 0.0) for i in n]
        p = [eye + pw[i] for i in n]
        pw = [_bdot(pw[i], pw[i]) for i in n]
        span = 2
        while 2 * span < c:
            st = [_bdot(jnp.concatenate([pw[i], p[i]], axis=0), pw[i]) for i in n]
            p = [p[i] + st[i][c:, :] for i in n]
            pw = [st[i][:c, :] for i in n]
            span *= 2
        t_inv = [p[i] + _bdot(p[i], pw[i]) for i in n]
        e_col = [jnp.exp(decay_col[i]) for i in n]
        uw = [_bdot(t_inv[i], jnp.concatenate([vc[i] * beta_col[i], k_beta[i] * e_col[i]], axis=1))
              for i in n]
        for i, (ci, h) in enumerate(items):
            decay_last = decay_col[i][c - 1:c, :]
            vs_ref[rows[i], sls[h]] = uw[i][:, :d]
            w_ref[rows[i], sls[h]] = uw[i][:, d:]
            qs_ref[rows[i], sls[h]] = qc[i] * e_col[i]
            ks_ref[rows[i], sls[h]] = kc[i] * jnp.exp(decay_last - decay_col[i])
            qk_ref[h, rows[i], :] = kq[i][c:, :] * gamma[i]
            dl_ref[ci * hb + h] = jnp.broadcast_to(jnp.exp(decay_last), (SUBLANES, d))
        return carry

    def state_body(ci, carry):
        rows = pl.ds(pl.multiple_of(ci * c, c), c)
        s = [s_ref[h] for h in heads]
        ws = [_bdot(jnp.concatenate([w_ref[rows, sls[h]], qs_ref[rows, sls[h]]], axis=0), s[h])
              for h in heads]
        v_new = [vs_ref[rows, sls[h]] - ws[h][:c, :] for h in heads]
        o_intra = [_bdot(qk_ref[h, rows, :], v_new[h]) for h in heads]
        kv = [_bdot_tn(ks_ref[rows, sls[h]], v_new[h]) for h in heads]
        for h in heads:
            s_ref[h] = s[h] * dl_ref[ci * hb + h][0:1, :] + kv[h]
            o = ws[h][c:, :] + o_intra[h]
            o = o * lax.rsqrt(jnp.mean(o * o, axis=-1, keepdims=True) + RMS_EPS) * nw
            zc = z_ref[rows, sls[h]]
            o = o * (zc * _sigmoid(zc))
            o_ref[rows, sls[h]] = o.astype(o_ref.dtype)
        return carry

    lax.fori_loop(0, nc // SOLVE_CHUNKS, solve_body, 0)
    lax.fori_loop(0, nc, state_body, 0)


def _gdn(proj, conv_w, a_t, b_t, a_log, dt_bias, norm_w, *, heads):
    seq = proj.shape[0]
    d = GDN_HEAD_DIM
    hb = _pick_tile(heads, (8, 6, 4, 2))
    tb = _pick_tile(seq, (512,))
    n_hb = heads // hb
    wblk = hb * d
    kernel = functools.partial(_gdn_kernel, tb=tb, hb=hb)

    def col(offset):
        return lambda h, t: (t, offset * n_hb + h)

    def wcol(offset):
        return lambda h, t: (0, offset * n_hb + h)

    return pl.pallas_call(
        kernel,
        out_shape=jax.ShapeDtypeStruct((seq, heads * d), BF16),
        grid=(n_hb, seq // tb),
        in_specs=[pl.BlockSpec((tb, wblk), col(0)),
                  pl.BlockSpec((tb, wblk), col(1)),
                  pl.BlockSpec((tb, wblk), col(2)),
                  pl.BlockSpec((tb, wblk), col(3)),
                  pl.BlockSpec((CONV_WIDTH, wblk), wcol(0)),
                  pl.BlockSpec((CONV_WIDTH, wblk), wcol(1)),
                  pl.BlockSpec((CONV_WIDTH, wblk), wcol(2)),
                  pl.BlockSpec((hb, tb // GDN_CHUNK, GDN_CHUNK), lambda h, t: (h, t, 0)),
                  pl.BlockSpec((hb, tb // GDN_CHUNK, GDN_CHUNK), lambda h, t: (h, t, 0)),
                  pl.BlockSpec((hb, 1, 1), lambda h, t: (h, 0, 0)),
                  pl.BlockSpec((hb, 1, 1), lambda h, t: (h, 0, 0)),
                  pl.BlockSpec((1, d), lambda h, t: (0, 0))],
        out_specs=pl.BlockSpec((tb, wblk), lambda h, t: (t, h)),
        scratch_shapes=[pltpu.VMEM((hb, d, d), F32),
                        pltpu.VMEM((SUBLANES, wblk), F32),
                        pltpu.VMEM((SUBLANES, wblk), F32),
                        pltpu.VMEM((SUBLANES, wblk), F32),
                        pltpu.VMEM((tb, wblk), F32),
                        pltpu.VMEM((tb, wblk), F32),
                        pltpu.VMEM((tb, wblk), F32),
                        pltpu.VMEM((tb, wblk), F32),
                        pltpu.VMEM((hb, tb, GDN_CHUNK), F32),
                        pltpu.VMEM((tb // GDN_CHUNK * hb, SUBLANES, d), F32),
                        pltpu.VMEM((3 * hb, SUBLANES + GDN_CHUNK, d), F32)],
        compiler_params=_params(("parallel", "arbitrary")),
        name="gdn",
    )(proj, proj, proj, proj, conv_w, conv_w, conv_w, a_t, b_t,
      a_log.reshape(heads, 1, 1), dt_bias.reshape(heads, 1, 1), norm_w.reshape(1, d))


def _gelu_tanh(y):
    return 0.5 * y * (1.0 + jnp.tanh(math.sqrt(2.0 / math.pi) * (y + 0.044715 * (y * y * y))))


def _s5_kernel(u_ref, bre_ref, bim_ref, cre_ref, cim_ref, are_ref, aim_ref, d_ref, wglu_ref,
               o_ref, xr_ref, xi_ref, str_ref, sti_ref, y_ref, wglu_bf_ref, *, tb, pitch, slabs):
    tiles = slabs * S5_TILES_PER_SLAB
    n_vreg = tiles // SUBLANES

    @pl.when(pl.program_id(0) == 0)
    def _():
        str_ref[...] = jnp.zeros_like(str_ref)
        sti_ref[...] = jnp.zeros_like(sti_ref)
        _cast_rows_to_bf16(wglu_ref, wglu_bf_ref, LANES)

    for s in range(slabs):
        ub = u_ref[:, s * LANES:(s + 1) * LANES].astype(BF16)
        r = jnp.dot(ub, bre_ref[s], preferred_element_type=F32)
        m = jnp.dot(ub, bim_ref[s], preferred_element_type=F32)
        for t4 in range(S5_TILES_PER_SLAB):
            lt = s * S5_TILES_PER_SLAB + t4
            xr_ref[lt * pitch:lt * pitch + tb, :] = r[:, t4 * LANES:(t4 + 1) * LANES]
            xi_ref[lt * pitch:lt * pitch + tb, :] = m[:, t4 * LANES:(t4 + 1) * LANES]

    ar = [are_ref[j] for j in range(n_vreg)]
    ai = [aim_ref[j] for j in range(n_vreg)]

    def step(t, carry):
        xr, xi = carry
        nxr, nxi = [], []
        for j in range(n_vreg):
            rows = pl.ds(j * SUBLANES * pitch + t, SUBLANES, stride=pitch)
            br = xr_ref[rows, :]
            bi = xi_ref[rows, :]
            r = ar[j] * xr[j] - ai[j] * xi[j] + br
            m = ar[j] * xi[j] + ai[j] * xr[j] + bi
            xr_ref[rows, :] = r
            xi_ref[rows, :] = m
            nxr.append(r)
            nxi.append(m)
        return tuple(nxr), tuple(nxi)

    x0 = (tuple(str_ref[j] for j in range(n_vreg)), tuple(sti_ref[j] for j in range(n_vreg)))
    xr_f, xi_f = lax.fori_loop(0, tb, step, x0)
    for j in range(n_vreg):
        str_ref[j] = xr_f[j]
        sti_ref[j] = xi_f[j]

    for s in range(slabs):
        acc = jnp.zeros((tb, LANES), F32)
        for t4 in range(S5_TILES_PER_SLAB):
            lt = s * S5_TILES_PER_SLAB + t4
            xr = xr_ref[lt * pitch:lt * pitch + tb, :].astype(BF16)
            xi = xi_ref[lt * pitch:lt * pitch + tb, :].astype(BF16)
            acc = acc + jnp.dot(xr, cre_ref[s, t4 * LANES:(t4 + 1) * LANES, :],
                                preferred_element_type=F32)
            acc = acc - jnp.dot(xi, cim_ref[s, t4 * LANES:(t4 + 1) * LANES, :],
                                preferred_element_type=F32)
        sl = slice(s * LANES, (s + 1) * LANES)
        y_ref[:, sl] = _gelu_tanh(acc + d_ref[:, sl] * u_ref[:, sl])

    y = y_ref[...]
    gate = jnp.dot(y.astype(BF16), wglu_bf_ref[...], preferred_element_type=F32)
    o_ref[...] = (y * _sigmoid(gate)).astype(o_ref.dtype)


def _s5(proj, u_col_block, lam_re, lam_im, b_re, b_im, c_re, c_im, d_skip, log_dt, w_glu):
    seq = proj.shape[0]
    groups = lam_re.shape[0]
    width = groups * S5_GROUP
    slabs = width // LANES
    tiles = slabs * S5_TILES_PER_SLAB
    assert tiles % SUBLANES == 0
    tb = _pick_tile(seq, (512,))
    pitch = tb + SUBLANES

    dt = jnp.exp(log_dt)[:, None]
    mag = jnp.exp(lam_re * dt)
    ab_re, ab_im = mag * jnp.cos(lam_im * dt), mag * jnp.sin(lam_im * dt)
    den = lam_re * lam_re + lam_im * lam_im
    nr, ni = ab_re - 1.0, ab_im
    coef_re = (nr * lam_re + ni * lam_im) / den
    coef_im = (ni * lam_re - nr * lam_im) / den
    bb_re = coef_re[..., None] * b_re - coef_im[..., None] * b_im
    bb_im = coef_re[..., None] * b_im + coef_im[..., None] * b_re
    eye = jnp.eye(S5_GROUPS_PER_SLAB, dtype=F32)

    def b_blockdiag(bb):
        t = bb.reshape(slabs, S5_GROUPS_PER_SLAB, S5_STATE, S5_GROUP)
        m = jnp.einsum('saph,ab->sahbp', t, eye)
        return m.reshape(slabs, LANES, S5_SLAB_STATES).astype(BF16)

    def c_blockdiag(cc):
        t = cc.reshape(slabs, S5_GROUPS_PER_SLAB, S5_GROUP, S5_STATE)
        m = jnp.einsum('sahp,ab->sapbh', t, eye)
        return m.reshape(slabs, S5_SLAB_STATES, LANES).astype(BF16)

    n_vreg = tiles // SUBLANES
    kernel = functools.partial(_s5_kernel, tb=tb, pitch=pitch, slabs=slabs)
    full3 = lambda i: (0, 0, 0)
    return pl.pallas_call(
        kernel,
        out_shape=jax.ShapeDtypeStruct((seq, width), BF16),
        grid=(seq // tb,),
        in_specs=[pl.BlockSpec((tb, width), lambda i: (i, u_col_block)),
                  pl.BlockSpec((slabs, LANES, S5_SLAB_STATES), full3),
                  pl.BlockSpec((slabs, LANES, S5_SLAB_STATES), full3),
                  pl.BlockSpec((slabs, S5_SLAB_STATES, LANES), full3),
                  pl.BlockSpec((slabs, S5_SLAB_STATES, LANES), full3),
                  pl.BlockSpec((n_vreg, SUBLANES, LANES), full3),
                  pl.BlockSpec((n_vreg, SUBLANES, LANES), full3),
                  pl.BlockSpec((1, width), lambda i: (0, 0)),
                  pl.BlockSpec((width, width), lambda i: (0, 0))],
        out_specs=pl.BlockSpec((tb, width), lambda i: (i, 0)),
        scratch_shapes=[pltpu.VMEM((tiles * pitch, LANES), F32),
                        pltpu.VMEM((tiles * pitch, LANES), F32),
                        pltpu.VMEM((n_vreg, SUBLANES, LANES), F32),
                        pltpu.VMEM((n_vreg, SUBLANES, LANES), F32),
                        pltpu.VMEM((tb, width), F32),
                        pltpu.VMEM((width, width), BF16)],
        compiler_params=_params(("arbitrary",)),
        name="s5",
    )(proj, b_blockdiag(bb_re), b_blockdiag(bb_im), c_blockdiag(c_re), c_blockdiag(c_im),
      ab_re.reshape(n_vreg, SUBLANES, LANES), ab_im.reshape(n_vreg, SUBLANES, LANES),
      d_skip.reshape(1, width), w_glu)


def _ln_rows(h, g_ref, b_ref):
    mu = jnp.mean(h, axis=-1, keepdims=True)
    hc = h - mu
    var = jnp.mean(hc * hc, axis=-1, keepdims=True)
    return hc * lax.rsqrt(var + LN_EPS) * g_ref[...] + b_ref[...]


def _pack_bf16_halves(y):
    half = y.shape[1] // 2
    bits = lax.bitcast_convert_type(y.astype(BF16).astype(F32), jnp.uint32)
    return (bits[:, :half] >> 16) | (bits[:, half:] & jnp.uint32(0xFFFF0000))


def _unpack_bf16_halves(p):
    lo = lax.bitcast_convert_type(p << 16, F32)
    hi = lax.bitcast_convert_type(p & jnp.uint32(0xFFFF0000), F32)
    return jnp.concatenate([lo, hi], axis=1)


def _ln_router_kernel(x_ref, m_ref, g_ref, b_ref, wr_ref, o_ref, op_ref, lg_ref, *, alpha):
    y = _ln_rows(alpha * x_ref[...] + m_ref[...], g_ref, b_ref)
    o_ref[...] = y
    op_ref[...] = _pack_bf16_halves(y)
    lg_ref[...] = jnp.dot(y, wr_ref[...], preferred_element_type=F32,
                          precision=lax.Precision.HIGHEST)


def _ln_pair_kernel(x_ref, p0_ref, p1_ref, g_ref, b_ref, o_ref, ob_ref, *, alpha):
    ffn = _unpack_bf16_halves(p0_ref[...]) + _unpack_bf16_halves(p1_ref[...])
    y = _ln_rows(alpha * x_ref[...] + ffn, g_ref, b_ref)
    o_ref[...] = y
    ob_ref[...] = y.astype(BF16)


def _residual_layer_norm(x, addends, g, b, alpha, w_router=None):
    seq, dm = x.shape
    tm = _pick_tile(seq, (256, 128))
    row = pl.BlockSpec((tm, dm), lambda i: (i, 0))
    half_row = pl.BlockSpec((tm, dm // 2), lambda i: (i, 0))
    vec = pl.BlockSpec((1, dm), lambda i: (0, 0))
    args = [x, *addends, g.reshape(1, dm), b.reshape(1, dm)]
    if w_router is None:
        assert len(addends) == 2
        kernel = functools.partial(_ln_pair_kernel, alpha=alpha)
        in_specs = [row, half_row, half_row, vec, vec]
        out_shape = [jax.ShapeDtypeStruct((seq, dm), F32), jax.ShapeDtypeStruct((seq, dm), BF16)]
        out_specs = [row, row]
    else:
        assert len(addends) == 1
        kernel = functools.partial(_ln_router_kernel, alpha=alpha)
        nr = w_router.shape[1]
        args.append(w_router)
        in_specs = [row, row, vec, vec, pl.BlockSpec((dm, nr), lambda i: (0, 0))]
        out_shape = [jax.ShapeDtypeStruct((seq, dm), F32),
                     jax.ShapeDtypeStruct((seq, dm // 2), jnp.uint32),
                     jax.ShapeDtypeStruct((seq, nr), F32)]
        out_specs = [row, half_row, pl.BlockSpec((tm, nr), lambda i: (i, 0))]
    return pl.pallas_call(
        kernel,
        out_shape=out_shape,
        grid=(seq // tm,),
        in_specs=in_specs,
        out_specs=out_specs,
        compiler_params=_params(("parallel",)),
        name="residual_layer_norm",
    )(*args)


def _moe_kernel(be_ref, first_ref, nxt_ref, nused_ref, x_ref, rw_ref, wg_hbm, wu_hbm, wd_hbm, o_ref,
                sg_ref, su_ref, sd_ref, bg_ref, bu_ref, bd_ref, sem, *, layer):
    b = pl.program_id(0)

    def weight_copies(e):
        return (pltpu.make_async_copy(wg_hbm.at[layer, e], sg_ref, sem.at[0]),
                pltpu.make_async_copy(wu_hbm.at[layer, e], su_ref, sem.at[1]),
                pltpu.make_async_copy(wd_hbm.at[layer, e], sd_ref, sem.at[2]))

    @pl.when(b == 0)
    def _():
        for cp in weight_copies(be_ref[0]):
            cp.start()

    @pl.when(first_ref[b] == 1)
    def _():
        for cp in weight_copies(be_ref[b]):
            cp.wait()
        _cast_rows_to_bf16(sg_ref, bg_ref, 512)
        _cast_rows_to_bf16(su_ref, bu_ref, 512)
        _cast_rows_to_bf16(sd_ref, bd_ref, 64)

        @pl.when(nxt_ref[b] >= 0)
        def _():
            for cp in weight_copies(nxt_ref[b]):
                cp.start()

    @pl.when(b >= nused_ref[0])
    def _():
        o_ref[...] = jnp.zeros_like(o_ref)

    @pl.when(b < nused_ref[0])
    def _():
        x = _unpack_bf16_halves(x_ref[...]).astype(BF16)
        g = jnp.dot(x, bg_ref[...], preferred_element_type=F32)
        u = jnp.dot(x, bu_ref[...], preferred_element_type=F32)
        hid = (g * _sigmoid(g)) * u
        y = jnp.dot(hid.astype(BF16), bd_ref[...], preferred_element_type=F32)
        o_ref[...] = _pack_bf16_halves(y * rw_ref[...])


def _moe_experts(xp, row_w, blk_expert, blk_first, blk_next, n_used, w_gate, w_up, w_down, layer,
                 *, bm):
    n_rows = xp.shape[0]
    dm, de = w_gate.shape[2:]
    n_blocks = n_rows // bm
    assert dm % 512 == 0 and de % 64 == 0
    grid_spec = pltpu.PrefetchScalarGridSpec(
        num_scalar_prefetch=4,
        grid=(n_blocks,),
        in_specs=[pl.BlockSpec((bm, dm // 2), lambda i, be, fi, nx, nu: (i, 0)),
                  pl.BlockSpec((bm, 1), lambda i, be, fi, nx, nu: (i, 0)),
                  pl.BlockSpec(memory_space=pl.ANY),
                  pl.BlockSpec(memory_space=pl.ANY),
                  pl.BlockSpec(memory_space=pl.ANY)],
        out_specs=pl.BlockSpec((bm, dm // 2), lambda i, be, fi, nx, nu: (i, 0)),
        scratch_shapes=[pltpu.VMEM((dm, de), F32), pltpu.VMEM((dm, de), F32),
                        pltpu.VMEM((de, dm), F32),
                        pltpu.VMEM((dm, de), BF16), pltpu.VMEM((dm, de), BF16),
                        pltpu.VMEM((de, dm), BF16),
                        pltpu.SemaphoreType.DMA((3,))],
    )
    return pl.pallas_call(
        functools.partial(_moe_kernel, layer=layer),
        out_shape=jax.ShapeDtypeStruct((n_rows, dm // 2), jnp.uint32),
        grid_spec=grid_spec,
        compiler_params=_params(("arbitrary",)),
        name="moe_experts",
    )(blk_expert, blk_first, blk_next, n_used, xp, row_w, w_gate, w_up, w_down)


def _moe(xp, logits, b_rg, b_re, w_gate, w_up, w_down, layer, *, bm):
    n_tok = xp.shape[0]
    n_experts = N_GROUPS * EXPERTS_PER_GROUP
    grp_logits = logits[:, :N_GROUPS] + b_rg
    exp_logits = logits[:, N_GROUPS:N_GROUPS + n_experts].reshape(n_tok, N_GROUPS, EXPERTS_PER_GROUP) + b_re
    grp_probs = jax.nn.softmax(grp_logits, axis=-1)
    grp_idx = jnp.argmax(grp_probs, axis=-1).astype(jnp.int32)[:, None]
    grp_p = jnp.max(grp_probs, axis=-1, keepdims=True)
    sel = exp_logits[:, 0]
    for g in range(1, N_GROUPS):
        sel = jnp.where(grp_idx == g, exp_logits[:, g], sel)
    lane = jnp.arange(EXPERTS_PER_GROUP, dtype=jnp.int32)[None, :]
    idx1 = jnp.argmax(sel, axis=-1).astype(jnp.int32)[:, None]
    rest = jnp.where(lane == idx1, -jnp.inf, sel)
    idx2 = jnp.argmax(rest, axis=-1).astype(jnp.int32)[:, None]
    top_idx = jnp.concatenate([idx1, idx2], axis=1)
    top_logit = jnp.concatenate([jnp.max(sel, axis=-1, keepdims=True),
                                 jnp.max(rest, axis=-1, keepdims=True)], axis=1)
    top_w = jax.nn.softmax(top_logit, axis=-1) * grp_p
    expert_id = (grp_idx * EXPERTS_PER_GROUP + top_idx).reshape(-1).astype(jnp.int32)
    n_assign = n_tok * TOP_K
    wts = top_w.reshape(-1)
    onehot = (expert_id[:, None] == jnp.arange(n_experts, dtype=jnp.int32)[None, :]).astype(jnp.int32)
    csum = jnp.cumsum(onehot, axis=0)
    counts = csum[-1]
    padded = (counts + bm - 1) // bm * bm
    pend = jnp.cumsum(padded)
    pstart = pend - padded
    dest = jnp.sum(onehot * (pstart[None, :] + csum - 1), axis=1)
    n_blocks = (n_assign + bm - 1) // bm + n_experts
    n_rows = n_blocks * bm
    row_asg = jnp.full((n_rows,), -1, jnp.int32).at[dest].set(jnp.arange(n_assign, dtype=jnp.int32))
    live = row_asg >= 0
    row_tok = jnp.where(live, row_asg // TOP_K, jnp.arange(n_rows, dtype=jnp.int32) % n_tok)
    row_w = jnp.where(live, wts[jnp.maximum(row_asg, 0)], 0.0)
    blk_start = jnp.arange(n_blocks, dtype=jnp.int32) * bm
    blk_expert = jnp.minimum(jnp.searchsorted(pend, blk_start, side='right'),
                             n_experts - 1).astype(jnp.int32)
    blk_first = jnp.concatenate([jnp.ones((1,), jnp.int32),
                                 (blk_expert[1:] != blk_expert[:-1]).astype(jnp.int32)])
    nxt_idx = jnp.searchsorted(blk_expert, blk_expert, side='right')
    blk_next = jnp.where(nxt_idx < n_blocks, blk_expert[jnp.minimum(nxt_idx, n_blocks - 1)],
                         -1).astype(jnp.int32)
    xg = xp[row_tok]
    n_used = (pend[-1:] // bm).astype(jnp.int32)
    yb = _moe_experts(xg, row_w[:, None], blk_expert, blk_first, blk_next, n_used,
                      w_gate, w_up, w_down, layer, bm=bm)
    pos = dest.reshape(n_tok, TOP_K)
    return yb[pos[:, 0]], yb[pos[:, 1]]


def kernel(x, w_in, gdn_conv_w, gdn_a_log, gdn_dt_bias, gdn_norm_w, s5_lambda_re, s5_lambda_im, s5_b_re, s5_b_im, s5_c_re, s5_c_im, s5_d, s5_log_dt, s5_w_glu, w_out, ln1_g, ln1_b, router_group_w, router_group_b, router_expert_w, router_expert_b, expert_w_gate, expert_w_up, expert_w_down, ln2_g, ln2_b):
    bsz, seq, dm = x.shape
    assert bsz == 1
    depth = w_in.shape[0]
    heads = gdn_a_log.shape[1]
    gdn_width = heads * GDN_HEAD_DIM
    s5_width = s5_d.shape[1]
    c_z = 4 * gdn_width
    c_b = c_z + 2 * heads
    alpha = (2 * depth) ** 0.25
    n_experts = N_GROUPS * EXPERTS_PER_GROUP
    router_cols = LANES
    moe_bm = 256
    tail_cols = s5_width + LANES

    w_in_t = jnp.swapaxes(w_in, 1, 2)
    xf = x.reshape(seq, dm)
    xb = xf.astype(BF16)
    for i in range(depth):
        w_rest = lax.slice(w_in_t, (i, c_z, 0), (i + 1, w_in_t.shape[1], dm))[0]
        w_tail = jnp.concatenate(
            [w_rest[2 * heads:], w_rest[:2 * heads], jnp.zeros((LANES - 2 * heads, dm), F32)], axis=0)
        proj = _matmul(xb, w_in_t, i, c_z, F32, transposed=True)
        proj_tail = _matmul(xb, w_tail[None], 0, tail_cols, F32, transposed=True)
        a_t = proj_tail[:, s5_width:s5_width + heads].T.reshape(heads, seq // GDN_CHUNK, GDN_CHUNK)
        b_t = proj_tail[:, s5_width + heads:s5_width + 2 * heads].T.reshape(
            heads, seq // GDN_CHUNK, GDN_CHUNK)
        y_gdn = _gdn(proj, gdn_conv_w[i], a_t, b_t, gdn_a_log[i], gdn_dt_bias[i], gdn_norm_w[i],
                     heads=heads)
        y_s5 = _s5(proj_tail, 0, s5_lambda_re[i], s5_lambda_im[i], s5_b_re[i], s5_b_im[i],
                   s5_c_re[i], s5_c_im[i], s5_d[i], s5_log_dt[i], s5_w_glu[i])
        mix = _matmul(jnp.concatenate([y_gdn, y_s5], axis=1), w_out, i, dm, F32)
        w_router = jnp.concatenate(
            [router_group_w[i],
             jnp.transpose(router_expert_w[i], (1, 0, 2)).reshape(dm, n_experts),
             jnp.zeros((dm, router_cols - N_GROUPS - n_experts), F32)], axis=1)
        xf, xp, logits = _residual_layer_norm(xf, (mix,), ln1_g[i], ln1_b[i], alpha, w_router)
        ffn_pair = _moe(xp, logits, router_group_b[i], router_expert_b[i],
                        expert_w_gate, expert_w_up, expert_w_down, i, bm=moe_bm)
        xf, xb = _residual_layer_norm(xf, ffn_pair, ln2_g[i], ln2_b[i], alpha)
    return xf.reshape(bsz, seq, dm)
```

```python
import functools
import math

import jax
import jax.numpy as jnp
from jax import lax
from jax.experimental import pallas as pl
from jax.experimental.pallas import tpu as pltpu

F32 = jnp.float32
BF16 = jnp.bfloat16

LANES = 128
SUBLANES = 8
VMEM_LIMIT_BYTES = 56 * 1024 * 1024

GDN_CHUNK = 64
GDN_HEAD_DIM = 128
SOLVE_CHUNKS = 8
CONV_WIDTH = 4
S5_GROUP = 16
S5_STATE = 64
S5_GROUPS_PER_SLAB = LANES // S5_GROUP
S5_SLAB_STATES = S5_GROUPS_PER_SLAB * S5_STATE
S5_TILES_PER_SLAB = S5_SLAB_STATES // LANES
N_GROUPS = 4
EXPERTS_PER_GROUP = 8
TOP_K = 2
LN_EPS = 1e-5
RMS_EPS = 1e-6
L2_EPS = 1e-6


def _params(sem):
    return pltpu.CompilerParams(dimension_semantics=sem, vmem_limit_bytes=VMEM_LIMIT_BYTES)


def _pick_tile(n, candidates):
    for c in candidates:
        if n % c == 0:
            return c
    return n


def _bdot(a, b):
    return jnp.dot(a.astype(BF16), b.astype(BF16), preferred_element_type=F32)


def _bdot_nt(a, b):
    return lax.dot_general(a.astype(BF16), b.astype(BF16), (((1,), (1,)), ((), ())),
                           preferred_element_type=F32)


def _bdot_tn(a, b):
    return lax.dot_general(a.astype(BF16), b.astype(BF16), (((0,), (0,)), ((), ())),
                           preferred_element_type=F32)


def _sigmoid(x):
    return 1.0 / (1.0 + jnp.exp(-x))


def _cast_rows_to_bf16(src_ref, dst_ref, rows_per_step):
    def body(r, carry):
        rows = pl.ds(pl.multiple_of(r * rows_per_step, rows_per_step), rows_per_step)
        dst_ref[rows, :] = src_ref[rows, :].astype(BF16)
        return carry

    lax.fori_loop(0, src_ref.shape[0] // rows_per_step, body, 0)


def _mm_kernel(a_ref, b_ref, o_ref, bb_ref, *, cast_rows, transposed):
    @pl.when(pl.program_id(1) == 0)
    def _():
        _cast_rows_to_bf16(b_ref, bb_ref, cast_rows)

    if transposed:
        acc = lax.dot_general(a_ref[...], bb_ref[...], (((1,), (1,)), ((), ())),
                              preferred_element_type=F32)
    else:
        acc = jnp.dot(a_ref[...], bb_ref[...], preferred_element_type=F32)
    o_ref[...] = acc.astype(o_ref.dtype)


def _matmul(a, w, layer, n_cols, out_dtype, *, transposed=False):
    m, kdim = a.shape
    tm = _pick_tile(m, (1024, 512, 256, 128))
    tn = _pick_tile(n_cols, (512, 384, 256, 128))
    if transposed:
        w_block, w_index, cast_rows = (None, tn, kdim), (lambda j, i: (layer, j, 0)), LANES
    else:
        w_block, w_index = (None, kdim, tn), (lambda j, i: (layer, 0, j))
        cast_rows = _pick_tile(kdim, (512, 256, 128))
    return pl.pallas_call(
        functools.partial(_mm_kernel, cast_rows=cast_rows, transposed=transposed),
        out_shape=jax.ShapeDtypeStruct((m, n_cols), out_dtype),
        grid=(n_cols // tn, m // tm),
        in_specs=[pl.BlockSpec((tm, kdim), lambda j, i: (i, 0)),
                  pl.BlockSpec(w_block, w_index)],
        out_specs=pl.BlockSpec((tm, tn), lambda j, i: (i, j)),
        scratch_shapes=[pltpu.VMEM(w_block[1:], BF16)],
        compiler_params=_params(("arbitrary", "arbitrary")),
        name="matmul",
    )(a, w)


def _gdn_kernel(q_ref, k_ref, v_ref, z_ref, wq_ref, wk_ref, wv_ref, a_ref, b_ref,
                alog_ref, dtb_ref, nw_ref, o_ref,
                s_ref, cq_ref, ck_ref, cv_ref, qs_ref, ks_ref, vs_ref, w_ref, qk_ref, dl_ref, ext_refs,
                *, tb, hb):
    c = GDN_CHUNK
    d = GDN_HEAD_DIM
    nc = tb // c

    @pl.when(pl.program_id(1) == 0)
    def _():
        s_ref[...] = jnp.zeros_like(s_ref)
        cq_ref[...] = jnp.zeros_like(cq_ref)
        ck_ref[...] = jnp.zeros_like(ck_ref)
        cv_ref[...] = jnp.zeros_like(cv_ref)

    def conv_body(ri, carry):
        r0 = pl.multiple_of(ri * c, c)
        rp = pl.multiple_of(jnp.maximum(r0 - SUBLANES, 0), SUBLANES)
        for a, (x_ref, c_ref, w_ref, dst_ref, scale) in enumerate(
                ((q_ref, cq_ref, wq_ref, qs_ref, d ** -0.5),
                 (k_ref, ck_ref, wk_ref, ks_ref, 1.0),
                 (v_ref, cv_ref, wv_ref, vs_ref, None))):
            for h in range(hb):
                sl = slice(h * d, (h + 1) * d)
                ext_ref = ext_refs.at[a * hb + h]
                ext_ref[0:SUBLANES, :] = jnp.where(ri == 0, c_ref[:, sl],
                                                   x_ref[pl.ds(rp, SUBLANES), sl])
                cur = x_ref[pl.ds(r0, c), sl]
                ext_ref[SUBLANES:, :] = cur
                w = w_ref[:, sl]
                acc = cur * w[CONV_WIDTH - 1:CONV_WIDTH, :]
                for j in range(1, CONV_WIDTH):
                    acc = acc + (ext_ref[SUBLANES - j:SUBLANES - j + c, :]
                                 * w[CONV_WIDTH - 1 - j:CONV_WIDTH - j, :])
                y = acc * _sigmoid(acc)
                if scale is not None:
                    y = y * (lax.rsqrt(jnp.sum(y * y, axis=-1, keepdims=True) + L2_EPS) * scale)
                dst_ref[pl.ds(r0, c), sl] = y
        return carry

    lax.fori_loop(0, nc, conv_body, 0)
    cq_ref[...] = q_ref[tb - SUBLANES:tb, :]
    ck_ref[...] = k_ref[tb - SUBLANES:tb, :]
    cv_ref[...] = v_ref[tb - SUBLANES:tb, :]

    ii = lax.broadcasted_iota(jnp.int32, (c, c), 0)
    jj = lax.broadcasted_iota(jnp.int32, (c, c), 1)
    causal = jj <= ii
    diag = jj == ii
    nw = nw_ref[...]

    heads = range(hb)
    sls = [slice(h * d, (h + 1) * d) for h in heads]

    ii2 = lax.broadcasted_iota(jnp.int32, (c, d), 0)
    ll2 = lax.broadcasted_iota(jnp.int32, (c, d), 1)
    left = ll2 < c
    jj2 = jnp.where(left, ll2, ll2 - c)
    causal2 = jj2 <= ii2
    strict2 = jj2 < ii2
    diag2 = jj2 == ii2
    eye2 = jnp.where(diag2, 1.0, 0.0).astype(F32)

    def block_diag(m):
        return jnp.concatenate([jnp.where(left, m, 0.0), jnp.where(left, 0.0, m)], axis=0)

    def side_by_side(a, b):
        z = jnp.zeros(a.shape, BF16)
        return jnp.concatenate([jnp.concatenate([a.astype(BF16), z], axis=1),
                                jnp.concatenate([z, b.astype(BF16)], axis=1)], axis=0)

    def solve_body(pi, carry):
        items = [(pi * SOLVE_CHUNKS + cj, h) for cj in range(SOLVE_CHUNKS) for h in heads]
        n = range(len(items))
        pairs = range(len(items) // 2)
        rows = [pl.ds(pl.multiple_of(ci * c, c), c) for ci, _ in items]
        qc = [qs_ref[rows[i], sls[h]] for i, (_, h) in enumerate(items)]
        kc = [ks_ref[rows[i], sls[h]] for i, (_, h) in enumerate(items)]
        vc = [vs_ref[rows[i], sls[h]] for i, (_, h) in enumerate(items)]
        decay_col, beta_col = [], []
        for ci, h in items:
            a_row = a_ref[h, pl.ds(ci, 1), :]
            b_row = b_ref[h, pl.ds(ci, 1), :]
            sp_in = a_row + dtb_ref[h]
            softplus = jnp.maximum(sp_in, 0.0) + jnp.log1p(jnp.exp(-jnp.abs(sp_in)))
            g_row = -jnp.exp(alog_ref[h]) * softplus
            beta_row = _sigmoid(b_row)
            g_b = jnp.broadcast_to(g_row, (c, c))
            decay_col.append(jnp.sum(jnp.where(causal, g_b, 0.0), axis=1, keepdims=True))
            beta_col.append(jnp.sum(jnp.where(diag, jnp.broadcast_to(beta_row, (c, c)), 0.0),
                                    axis=1, keepdims=True))
        gamma = []
        for j in pairs:
            dsel = jnp.where(left, decay_col[2 * j], decay_col[2 * j + 1])
            drow = jnp.sum(jnp.where(diag2, dsel, 0.0), axis=0, keepdims=True)
            gamma.append(jnp.where(causal2, jnp.exp(jnp.where(causal2, dsel - drow, 0.0)), 0.0))
        k_beta = [kc[i] * beta_col[i] for i in n]
        kq = [_bdot_nt(jnp.concatenate(
                  [jnp.concatenate([k_beta[2 * j].astype(BF16), k_beta[2 * j + 1].astype(BF16)], axis=1),
                   jnp.concatenate([qc[2 * j].astype(BF16), qc[2 * j + 1].astype(BF16)], axis=1)], axis=0),
                  side_by_side(kc[2 * j], kc[2 * j + 1])) for j in pairs]
        pw = [jnp.where(strict2, -(kq[j][:c, :] * gamma[j]), 0.0) for j in pairs]
        p = [eye2 + pw[j] for j in pairs]
        pw = [_bdot(pw[j], block_diag(pw[j])) for j in pairs]
        span = 2
        while 2 * span < c:
            st = [_bdot(jnp.concatenate([pw[j], p[j]], axis=0), block_diag(pw[j])) for j in pairs]
            p = [p[j] + st[j][c:, :] for j in pairs]
            pw = [st[j][:c, :] for j in pairs]
            span *= 2
        t_inv = [p[j] + _bdot(p[j], block_diag(pw[j])) for j in pairs]
        e_col = [jnp.exp(decay_col[i]) for i in n]
        rhs = [jnp.concatenate([(vc[i] * beta_col[i]).astype(BF16),
                                (k_beta[i] * e_col[i]).astype(BF16)], axis=1) for i in n]
        zero2 = jnp.zeros((c, 2 * d), BF16)
        uw = [_bdot(t_inv[j], jnp.concatenate(
                  [jnp.concatenate([rhs[2 * j], zero2], axis=1),
                   jnp.concatenate([zero2, rhs[2 * j + 1]], axis=1)], axis=0))
              for j in pairs]
        for i, (ci, h) in enumerate(items):
            j, off = i // 2, (i % 2) * 2 * d
            decay_last = decay_col[i][c - 1:c, :]
            vs_ref[rows[i], sls[h]] = uw[j][:, off:off + d]
            w_ref[rows[i], sls[h]] = uw[j][:, off + d:off + 2 * d]
            qs_ref[rows[i], sls[h]] = qc[i] * e_col[i]
            ks_ref[rows[i], sls[h]] = kc[i] * jnp.exp(decay_last - decay_col[i])
            dl_ref[ci * hb + h] = jnp.broadcast_to(jnp.exp(decay_last), (SUBLANES, d))
            if i % 2 == 0:
                qk_ref[h // 2, rows[i], :] = kq[j][c:, :] * gamma[j]
        return carry

    def state_body(ci, carry):
        rows = pl.ds(pl.multiple_of(ci * c, c), c)
        s = [s_ref[h] for h in heads]
        ws = [_bdot(jnp.concatenate([w_ref[rows, sls[h]], qs_ref[rows, sls[h]]], axis=0), s[h])
              for h in heads]
        v_new = [vs_ref[rows, sls[h]] - ws[h][:c, :] for h in heads]
        o_intra = [_bdot(qk_ref[j, rows, :], side_by_side(v_new[2 * j], v_new[2 * j + 1]))
                   for j in range(hb // 2)]
        kv = [_bdot_tn(ks_ref[rows, sls[h]], v_new[h]) for h in heads]
        for h in heads:
            s_ref[h] = s[h] * dl_ref[ci * hb + h][0:1, :] + kv[h]
            o = ws[h][c:, :] + o_intra[h // 2][:, (h % 2) * d:(h % 2 + 1) * d]
            o = o * lax.rsqrt(jnp.mean(o * o, axis=-1, keepdims=True) + RMS_EPS) * nw
            zc = z_ref[rows, sls[h]]
            o = o * (zc * _sigmoid(zc))
            o_ref[rows, sls[h]] = o.astype(o_ref.dtype)
        return carry

    lax.fori_loop(0, nc // SOLVE_CHUNKS, solve_body, 0)
    lax.fori_loop(0, nc, state_body, 0)


def _gdn(proj, conv_w, a_t, b_t, a_log, dt_bias, norm_w, *, heads):
    seq = proj.shape[0]
    d = GDN_HEAD_DIM
    hb = _pick_tile(heads, (8, 6, 4, 2))
    assert hb % 2 == 0 and d == 2 * GDN_CHUNK
    tb = _pick_tile(seq, (512,))
    n_hb = heads // hb
    wblk = hb * d
    kernel = functools.partial(_gdn_kernel, tb=tb, hb=hb)

    def col(offset):
        return lambda h, t: (t, offset * n_hb + h)

    def wcol(offset):
        return lambda h, t: (0, offset * n_hb + h)

    return pl.pallas_call(
        kernel,
        out_shape=jax.ShapeDtypeStruct((seq, heads * d), BF16),
        grid=(n_hb, seq // tb),
        in_specs=[pl.BlockSpec((tb, wblk), col(0)),
                  pl.BlockSpec((tb, wblk), col(1)),
                  pl.BlockSpec((tb, wblk), col(2)),
                  pl.BlockSpec((tb, wblk), col(3)),
                  pl.BlockSpec((CONV_WIDTH, wblk), wcol(0)),
                  pl.BlockSpec((CONV_WIDTH, wblk), wcol(1)),
                  pl.BlockSpec((CONV_WIDTH, wblk), wcol(2)),
                  pl.BlockSpec((hb, tb // GDN_CHUNK, GDN_CHUNK), lambda h, t: (h, t, 0)),
                  pl.BlockSpec((hb, tb // GDN_CHUNK, GDN_CHUNK), lambda h, t: (h, t, 0)),
                  pl.BlockSpec((hb, 1, 1), lambda h, t: (h, 0, 0)),
                  pl.BlockSpec((hb, 1, 1), lambda h, t: (h, 0, 0)),
                  pl.BlockSpec((1, d), lambda h, t: (0, 0))],
        out_specs=pl.BlockSpec((tb, wblk), lambda h, t: (t, h)),
        scratch_shapes=[pltpu.VMEM((hb, d, d), F32),
                        pltpu.VMEM((SUBLANES, wblk), F32),
                        pltpu.VMEM((SUBLANES, wblk), F32),
                        pltpu.VMEM((SUBLANES, wblk), F32),
                        pltpu.VMEM((tb, wblk), F32),
                        pltpu.VMEM((tb, wblk), F32),
                        pltpu.VMEM((tb, wblk), F32),
                        pltpu.VMEM((tb, wblk), F32),
                        pltpu.VMEM((hb // 2, tb, d), F32),
                        pltpu.VMEM((tb // GDN_CHUNK * hb, SUBLANES, d), F32),
                        pltpu.VMEM((3 * hb, SUBLANES + GDN_CHUNK, d), F32)],
        compiler_params=_params(("parallel", "arbitrary")),
        name="gdn",
    )(proj, proj, proj, proj, conv_w, conv_w, conv_w, a_t, b_t,
      a_log.reshape(heads, 1, 1), dt_bias.reshape(heads, 1, 1), norm_w.reshape(1, d))


def _gelu_tanh(y):
    return 0.5 * y * (1.0 + jnp.tanh(math.sqrt(2.0 / math.pi) * (y + 0.044715 * (y * y * y))))


def _s5_kernel(u_ref, bre_ref, bim_ref, cre_ref, cim_ref, are_ref, aim_ref, d_ref, wglu_ref,
               o_ref, xr_ref, xi_ref, str_ref, sti_ref, y_ref, wglu_bf_ref, *, tb, pitch, slabs):
    tiles = slabs * S5_TILES_PER_SLAB
    n_vreg = tiles // SUBLANES

    @pl.when(pl.program_id(0) == 0)
    def _():
        str_ref[...] = jnp.zeros_like(str_ref)
        sti_ref[...] = jnp.zeros_like(sti_ref)
        _cast_rows_to_bf16(wglu_ref, wglu_bf_ref, LANES)

    for s in range(slabs):
        ub = u_ref[:, s * LANES:(s + 1) * LANES].astype(BF16)
        r = jnp.dot(ub, bre_ref[s], preferred_element_type=F32)
        m = jnp.dot(ub, bim_ref[s], preferred_element_type=F32)
        for t4 in range(S5_TILES_PER_SLAB):
            lt = s * S5_TILES_PER_SLAB + t4
            xr_ref[lt * pitch:lt * pitch + tb, :] = r[:, t4 * LANES:(t4 + 1) * LANES]
            xi_ref[lt * pitch:lt * pitch + tb, :] = m[:, t4 * LANES:(t4 + 1) * LANES]

    ar = [are_ref[j] for j in range(n_vreg)]
    ai = [aim_ref[j] for j in range(n_vreg)]

    def step(t, carry):
        xr, xi = carry
        nxr, nxi = [], []
        for j in range(n_vreg):
            rows = pl.ds(j * SUBLANES * pitch + t, SUBLANES, stride=pitch)
            br = xr_ref[rows, :]
            bi = xi_ref[rows, :]
            r = ar[j] * xr[j] - ai[j] * xi[j] + br
            m = ar[j] * xi[j] + ai[j] * xr[j] + bi
            xr_ref[rows, :] = r
            xi_ref[rows, :] = m
            nxr.append(r)
            nxi.append(m)
        return tuple(nxr), tuple(nxi)

    x0 = (tuple(str_ref[j] for j in range(n_vreg)), tuple(sti_ref[j] for j in range(n_vreg)))
    xr_f, xi_f = lax.fori_loop(0, tb, step, x0, unroll=4)
    for j in range(n_vreg):
        str_ref[j] = xr_f[j]
        sti_ref[j] = xi_f[j]

    for s in range(slabs):
        acc = jnp.zeros((tb, LANES), F32)
        for t4 in range(S5_TILES_PER_SLAB):
            lt = s * S5_TILES_PER_SLAB + t4
            xr = xr_ref[lt * pitch:lt * pitch + tb, :].astype(BF16)
            xi = xi_ref[lt * pitch:lt * pitch + tb, :].astype(BF16)
            acc = acc + jnp.dot(xr, cre_ref[s, t4 * LANES:(t4 + 1) * LANES, :],
                                preferred_element_type=F32)
            acc = acc - jnp.dot(xi, cim_ref[s, t4 * LANES:(t4 + 1) * LANES, :],
                                preferred_element_type=F32)
        sl = slice(s * LANES, (s + 1) * LANES)
        y_ref[:, sl] = _gelu_tanh(acc + d_ref[:, sl] * u_ref[:, sl])

    y = y_ref[...]
    gate = jnp.dot(y.astype(BF16), wglu_bf_ref[...], preferred_element_type=F32)
    o_ref[...] = (y * _sigmoid(gate)).astype(o_ref.dtype)


def _s5(proj, u_col_block, lam_re, lam_im, b_re, b_im, c_re, c_im, d_skip, log_dt, w_glu):
    seq = proj.shape[0]
    groups = lam_re.shape[0]
    width = groups * S5_GROUP
    slabs = width // LANES
    tiles = slabs * S5_TILES_PER_SLAB
    assert tiles % SUBLANES == 0
    tb = _pick_tile(seq, (512,))
    pitch = tb + SUBLANES

    dt = jnp.exp(log_dt)[:, None]
    mag = jnp.exp(lam_re * dt)
    ab_re, ab_im = mag * jnp.cos(lam_im * dt), mag * jnp.sin(lam_im * dt)
    den = lam_re * lam_re + lam_im * lam_im
    nr, ni = ab_re - 1.0, ab_im
    coef_re = (nr * lam_re + ni * lam_im) / den
    coef_im = (ni * lam_re - nr * lam_im) / den
    bb_re = coef_re[..., None] * b_re - coef_im[..., None] * b_im
    bb_im = coef_re[..., None] * b_im + coef_im[..., None] * b_re
    eye = jnp.eye(S5_GROUPS_PER_SLAB, dtype=F32)

    def b_blockdiag(bb):
        t = bb.reshape(slabs, S5_GROUPS_PER_SLAB, S5_STATE, S5_GROUP)
        m = jnp.einsum('saph,ab->sahbp', t, eye)
        return m.reshape(slabs, LANES, S5_SLAB_STATES).astype(BF16)

    def c_blockdiag(cc):
        t = cc.reshape(slabs, S5_GROUPS_PER_SLAB, S5_GROUP, S5_STATE)
        m = jnp.einsum('sahp,ab->sapbh', t, eye)
        return m.reshape(slabs, S5_SLAB_STATES, LANES).astype(BF16)

    n_vreg = tiles // SUBLANES
    kernel = functools.partial(_s5_kernel, tb=tb, pitch=pitch, slabs=slabs)
    full3 = lambda i: (0, 0, 0)
    return pl.pallas_call(
        kernel,
        out_shape=jax.ShapeDtypeStruct((seq, width), BF16),
        grid=(seq // tb,),
        in_specs=[pl.BlockSpec((tb, width), lambda i: (i, u_col_block)),
                  pl.BlockSpec((slabs, LANES, S5_SLAB_STATES), full3),
                  pl.BlockSpec((slabs, LANES, S5_SLAB_STATES), full3),
                  pl.BlockSpec((slabs, S5_SLAB_STATES, LANES), full3),
                  pl.BlockSpec((slabs, S5_SLAB_STATES, LANES), full3),
                  pl.BlockSpec((n_vreg, SUBLANES, LANES), full3),
                  pl.BlockSpec((n_vreg, SUBLANES, LANES), full3),
                  pl.BlockSpec((1, width), lambda i: (0, 0)),
                  pl.BlockSpec((width, width), lambda i: (0, 0))],
        out_specs=pl.BlockSpec((tb, width), lambda i: (i, 0)),
        scratch_shapes=[pltpu.VMEM((tiles * pitch, LANES), F32),
                        pltpu.VMEM((tiles * pitch, LANES), F32),
                        pltpu.VMEM((n_vreg, SUBLANES, LANES), F32),
                        pltpu.VMEM((n_vreg, SUBLANES, LANES), F32),
                        pltpu.VMEM((tb, width), F32),
                        pltpu.VMEM((width, width), BF16)],
        compiler_params=_params(("arbitrary",)),
        name="s5",
    )(proj, b_blockdiag(bb_re), b_blockdiag(bb_im), c_blockdiag(c_re), c_blockdiag(c_im),
      ab_re.reshape(n_vreg, SUBLANES, LANES), ab_im.reshape(n_vreg, SUBLANES, LANES),
      d_skip.reshape(1, width), w_glu)


def _ln_rows(h, g_ref, b_ref):
    mu = jnp.mean(h, axis=-1, keepdims=True)
    hc = h - mu
    var = jnp.mean(hc * hc, axis=-1, keepdims=True)
    return hc * lax.rsqrt(var + LN_EPS) * g_ref[...] + b_ref[...]


def _pack_bf16_halves(y):
    half = y.shape[1] // 2
    bits = lax.bitcast_convert_type(y.astype(BF16).astype(F32), jnp.uint32)
    return (bits[:, :half] >> 16) | (bits[:, half:] & jnp.uint32(0xFFFF0000))


def _unpack_bf16_halves(p):
    lo = lax.bitcast_convert_type(p << 16, F32)
    hi = lax.bitcast_convert_type(p & jnp.uint32(0xFFFF0000), F32)
    return jnp.concatenate([lo, hi], axis=1)


def _ln_router_kernel(x_ref, m_ref, g_ref, b_ref, wr_ref, o_ref, op_ref, lg_ref, *, alpha):
    y = _ln_rows(alpha * x_ref[...] + m_ref[...], g_ref, b_ref)
    o_ref[...] = y
    op_ref[...] = _pack_bf16_halves(y)
    lg_ref[...] = jnp.dot(y, wr_ref[...], preferred_element_type=F32,
                          precision=lax.Precision.HIGHEST)


def _ln_pair_kernel(x_ref, p0_ref, p1_ref, g_ref, b_ref, o_ref, ob_ref, *, alpha):
    ffn = _unpack_bf16_halves(p0_ref[...]) + _unpack_bf16_halves(p1_ref[...])
    y = _ln_rows(alpha * x_ref[...] + ffn, g_ref, b_ref)
    o_ref[...] = y
    ob_ref[...] = y.astype(BF16)


def _residual_layer_norm(x, addends, g, b, alpha, w_router=None):
    seq, dm = x.shape
    tm = _pick_tile(seq, (256, 128))
    row = pl.BlockSpec((tm, dm), lambda i: (i, 0))
    half_row = pl.BlockSpec((tm, dm // 2), lambda i: (i, 0))
    vec = pl.BlockSpec((1, dm), lambda i: (0, 0))
    args = [x, *addends, g.reshape(1, dm), b.reshape(1, dm)]
    if w_router is None:
        assert len(addends) == 2
        kernel = functools.partial(_ln_pair_kernel, alpha=alpha)
        in_specs = [row, half_row, half_row, vec, vec]
        out_shape = [jax.ShapeDtypeStruct((seq, dm), F32), jax.ShapeDtypeStruct((seq, dm), BF16)]
        out_specs = [row, row]
    else:
        assert len(addends) == 1
        kernel = functools.partial(_ln_router_kernel, alpha=alpha)
        nr = w_router.shape[1]
        args.append(w_router)
        in_specs = [row, row, vec, vec, pl.BlockSpec((dm, nr), lambda i: (0, 0))]
        out_shape = [jax.ShapeDtypeStruct((seq, dm), F32),
                     jax.ShapeDtypeStruct((seq, dm // 2), jnp.uint32),
                     jax.ShapeDtypeStruct((seq, nr), F32)]
        out_specs = [row, half_row, pl.BlockSpec((tm, nr), lambda i: (i, 0))]
    return pl.pallas_call(
        kernel,
        out_shape=out_shape,
        grid=(seq // tm,),
        in_specs=in_specs,
        out_specs=out_specs,
        compiler_params=_params(("parallel",)),
        name="residual_layer_norm",
    )(*args)


def _moe_kernel(be_ref, first_ref, nxt_ref, nused_ref, x_ref, rw_ref, wg_hbm, wu_hbm, wd_hbm, o_ref,
                sg_ref, su_ref, sd_ref, bg_ref, bu_ref, bd_ref, sem, *, layer):
    b = pl.program_id(0)

    def weight_copies(e):
        return (pltpu.make_async_copy(wg_hbm.at[layer, e], sg_ref, sem.at[0]),
                pltpu.make_async_copy(wu_hbm.at[layer, e], su_ref, sem.at[1]),
                pltpu.make_async_copy(wd_hbm.at[layer, e], sd_ref, sem.at[2]))

    @pl.when(b == 0)
    def _():
        for cp in weight_copies(be_ref[0]):
            cp.start()

    @pl.when(first_ref[b] == 1)
    def _():
        for cp in weight_copies(be_ref[b]):
            cp.wait()
        _cast_rows_to_bf16(sg_ref, bg_ref, 512)
        _cast_rows_to_bf16(su_ref, bu_ref, 512)
        _cast_rows_to_bf16(sd_ref, bd_ref, 64)

        @pl.when(nxt_ref[b] >= 0)
        def _():
            for cp in weight_copies(nxt_ref[b]):
                cp.start()

    @pl.when(b >= nused_ref[0])
    def _():
        o_ref[...] = jnp.zeros_like(o_ref)

    @pl.when(b < nused_ref[0])
    def _():
        x = _unpack_bf16_halves(x_ref[...]).astype(BF16)
        g = jnp.dot(x, bg_ref[...], preferred_element_type=F32)
        u = jnp.dot(x, bu_ref[...], preferred_element_type=F32)
        hid = (g * _sigmoid(g)) * u
        y = jnp.dot(hid.astype(BF16), bd_ref[...], preferred_element_type=F32)
        o_ref[...] = _pack_bf16_halves(y * rw_ref[...])


def _moe_experts(xp, row_w, blk_expert, blk_first, blk_next, n_used, w_gate, w_up, w_down, layer,
                 *, bm):
    n_rows = xp.shape[0]
    dm, de = w_gate.shape[2:]
    n_blocks = n_rows // bm
    assert dm % 512 == 0 and de % 64 == 0
    grid_spec = pltpu.PrefetchScalarGridSpec(
        num_scalar_prefetch=4,
        grid=(n_blocks,),
        in_specs=[pl.BlockSpec((bm, dm // 2), lambda i, be, fi, nx, nu: (i, 0)),
                  pl.BlockSpec((bm, 1), lambda i, be, fi, nx, nu: (i, 0)),
                  pl.BlockSpec(memory_space=pl.ANY),
                  pl.BlockSpec(memory_space=pl.ANY),
                  pl.BlockSpec(memory_space=pl.ANY)],
        out_specs=pl.BlockSpec((bm, dm // 2), lambda i, be, fi, nx, nu: (i, 0)),
        scratch_shapes=[pltpu.VMEM((dm, de), F32), pltpu.VMEM((dm, de), F32),
                        pltpu.VMEM((de, dm), F32),
                        pltpu.VMEM((dm, de), BF16), pltpu.VMEM((dm, de), BF16),
                        pltpu.VMEM((de, dm), BF16),
                        pltpu.SemaphoreType.DMA((3,))],
    )
    return pl.pallas_call(
        functools.partial(_moe_kernel, layer=layer),
        out_shape=jax.ShapeDtypeStruct((n_rows, dm // 2), jnp.uint32),
        grid_spec=grid_spec,
        compiler_params=_params(("arbitrary",)),
        name="moe_experts",
    )(blk_expert, blk_first, blk_next, n_used, xp, row_w, w_gate, w_up, w_down)


def _moe(xp, logits, b_rg, b_re, w_gate, w_up, w_down, layer, *, bm):
    n_tok = xp.shape[0]
    n_experts = N_GROUPS * EXPERTS_PER_GROUP
    grp_logits = logits[:, :N_GROUPS] + b_rg
    exp_logits = logits[:, N_GROUPS:N_GROUPS + n_experts].reshape(n_tok, N_GROUPS, EXPERTS_PER_GROUP) + b_re
    grp_probs = jax.nn.softmax(grp_logits, axis=-1)
    grp_idx = jnp.argmax(grp_probs, axis=-1).astype(jnp.int32)[:, None]
    grp_p = jnp.max(grp_probs, axis=-1, keepdims=True)
    sel = exp_logits[:, 0]
    for g in range(1, N_GROUPS):
        sel = jnp.where(grp_idx == g, exp_logits[:, g], sel)
    lane = jnp.arange(EXPERTS_PER_GROUP, dtype=jnp.int32)[None, :]
    idx1 = jnp.argmax(sel, axis=-1).astype(jnp.int32)[:, None]
    rest = jnp.where(lane == idx1, -jnp.inf, sel)
    idx2 = jnp.argmax(rest, axis=-1).astype(jnp.int32)[:, None]
    top_idx = jnp.concatenate([idx1, idx2], axis=1)
    top_logit = jnp.concatenate([jnp.max(sel, axis=-1, keepdims=True),
                                 jnp.max(rest, axis=-1, keepdims=True)], axis=1)
    top_w = jax.nn.softmax(top_logit, axis=-1) * grp_p
    expert_id = (grp_idx * EXPERTS_PER_GROUP + top_idx).reshape(-1).astype(jnp.int32)
    n_assign = n_tok * TOP_K
    wts = top_w.reshape(-1)
    onehot = (expert_id[:, None] == jnp.arange(n_experts, dtype=jnp.int32)[None, :]).astype(jnp.int32)
    csum = jnp.cumsum(onehot, axis=0)
    counts = csum[-1]
    padded = (counts + bm - 1) // bm * bm
    pend = jnp.cumsum(padded)
    pstart = pend - padded
    dest = jnp.sum(onehot * (pstart[None, :] + csum - 1), axis=1)
    n_blocks = (n_assign + bm - 1) // bm + n_experts
    n_rows = n_blocks * bm
    blk_start = jnp.arange(n_blocks, dtype=jnp.int32) * bm
    blk_expert = jnp.minimum(jnp.searchsorted(pend, blk_start, side='right'),
                             n_experts - 1).astype(jnp.int32)
    order = jnp.argsort(expert_id, stable=True).astype(jnp.int32)
    row = jnp.arange(n_rows, dtype=jnp.int32)
    row_e = blk_expert[row // bm]
    row_k = row - pstart[row_e]
    live = row_k < counts[row_e]
    row_asg = order[jnp.clip((jnp.cumsum(counts) - counts)[row_e] + row_k, 0, n_assign - 1)]
    row_tok = jnp.where(live, row_asg // TOP_K, row % n_tok)
    row_w = jnp.where(live, wts[row_asg], 0.0)
    blk_first = jnp.concatenate([jnp.ones((1,), jnp.int32),
                                 (blk_expert[1:] != blk_expert[:-1]).astype(jnp.int32)])
    nxt_idx = jnp.searchsorted(blk_expert, blk_expert, side='right')
    blk_next = jnp.where(nxt_idx < n_blocks, blk_expert[jnp.minimum(nxt_idx, n_blocks - 1)],
                         -1).astype(jnp.int32)
    xg = xp[row_tok]
    n_used = (pend[-1:] // bm).astype(jnp.int32)
    yb = _moe_experts(xg, row_w[:, None], blk_expert, blk_first, blk_next, n_used,
                      w_gate, w_up, w_down, layer, bm=bm)
    pos = dest.reshape(n_tok, TOP_K)
    return yb[pos[:, 0]], yb[pos[:, 1]]


def kernel(x, w_in, gdn_conv_w, gdn_a_log, gdn_dt_bias, gdn_norm_w, s5_lambda_re, s5_lambda_im, s5_b_re, s5_b_im, s5_c_re, s5_c_im, s5_d, s5_log_dt, s5_w_glu, w_out, ln1_g, ln1_b, router_group_w, router_group_b, router_expert_w, router_expert_b, expert_w_gate, expert_w_up, expert_w_down, ln2_g, ln2_b):
    bsz, seq, dm = x.shape
    assert bsz == 1
    depth = w_in.shape[0]
    heads = gdn_a_log.shape[1]
    gdn_width = heads * GDN_HEAD_DIM
    s5_width = s5_d.shape[1]
    c_z = 4 * gdn_width
    c_b = c_z + 2 * heads
    alpha = (2 * depth) ** 0.25
    n_experts = N_GROUPS * EXPERTS_PER_GROUP
    router_cols = LANES
    moe_bm = 256
    tail_cols = s5_width + LANES

    w_in_t = jnp.swapaxes(w_in, 1, 2)
    xf = x.reshape(seq, dm)
    xb = xf.astype(BF16)
    for i in range(depth):
        w_rest = lax.slice(w_in_t, (i, c_z, 0), (i + 1, w_in_t.shape[1], dm))[0]
        w_tail = jnp.concatenate(
            [w_rest[2 * heads:], w_rest[:2 * heads], jnp.zeros((LANES - 2 * heads, dm), F32)], axis=0)
        proj = _matmul(xb, w_in_t, i, c_z, F32, transposed=True)
        proj_tail = _matmul(xb, w_tail[None], 0, tail_cols, F32, transposed=True)
        a_t = proj_tail[:, s5_width:s5_width + heads].T.reshape(heads, seq // GDN_CHUNK, GDN_CHUNK)
        b_t = proj_tail[:, s5_width + heads:s5_width + 2 * heads].T.reshape(
            heads, seq // GDN_CHUNK, GDN_CHUNK)
        y_gdn = _gdn(proj, gdn_conv_w[i], a_t, b_t, gdn_a_log[i], gdn_dt_bias[i], gdn_norm_w[i],
                     heads=heads)
        y_s5 = _s5(proj_tail, 0, s5_lambda_re[i], s5_lambda_im[i], s5_b_re[i], s5_b_im[i],
                   s5_c_re[i], s5_c_im[i], s5_d[i], s5_log_dt[i], s5_w_glu[i])
        mix = _matmul(jnp.concatenate([y_gdn, y_s5], axis=1), w_out, i, dm, F32)
        w_router = jnp.concatenate(
            [router_group_w[i],
             jnp.transpose(router_expert_w[i], (1, 0, 2)).reshape(dm, n_experts),
             jnp.zeros((dm, router_cols - N_GROUPS - n_experts), F32)], axis=1)
        xf, xp, logits = _residual_layer_norm(xf, (mix,), ln1_g[i], ln1_b[i], alpha, w_router)
        ffn_pair = _moe(xp, logits, router_group_b[i], router_expert_b[i],
                        expert_w_gate, expert_w_up, expert_w_down, i, bm=moe_bm)
        xf, xb = _residual_layer_norm(xf, ffn_pair, ln2_g[i], ln2_b[i], alpha)
    return xf.reshape(bsz, seq, dm)
```

```python
import functools
import math

import jax
import jax.numpy as jnp
from jax import lax
from jax.experimental import pallas as pl
from jax.experimental.pallas import tpu as pltpu

F32 = jnp.float32
BF16 = jnp.bfloat16

LANES = 128
SUBLANES = 8
VMEM_LIMIT_BYTES = 56 * 1024 * 1024

GDN_CHUNK = 64
GDN_HEAD_DIM = 128
SOLVE_CHUNKS = 8
CONV_WIDTH = 4
S5_GROUP = 16
S5_STATE = 64
S5_GROUPS_PER_SLAB = LANES // S5_GROUP
S5_SLAB_STATES = S5_GROUPS_PER_SLAB * S5_STATE
S5_TILES_PER_SLAB = S5_SLAB_STATES // LANES
N_GROUPS = 4
EXPERTS_PER_GROUP = 8
TOP_K = 2
LN_EPS = 1e-5
RMS_EPS = 1e-6
L2_EPS = 1e-6


def _params(sem):
    return pltpu.CompilerParams(dimension_semantics=sem, vmem_limit_bytes=VMEM_LIMIT_BYTES)


def _pick_tile(n, candidates):
    for c in candidates:
        if n % c == 0:
            return c
    return n


def _bdot(a, b):
    return jnp.dot(a.astype(BF16), b.astype(BF16), preferred_element_type=F32)


def _bdot_nt(a, b):
    return lax.dot_general(a.astype(BF16), b.astype(BF16), (((1,), (1,)), ((), ())),
                           preferred_element_type=F32)


def _bdot_tn(a, b):
    return lax.dot_general(a.astype(BF16), b.astype(BF16), (((0,), (0,)), ((), ())),
                           preferred_element_type=F32)


def _sigmoid(x):
    return 1.0 / (1.0 + jnp.exp(-x))


def _cast_rows_to_bf16(src_ref, dst_ref, rows_per_step):
    def body(r, carry):
        rows = pl.ds(pl.multiple_of(r * rows_per_step, rows_per_step), rows_per_step)
        dst_ref[rows, :] = src_ref[rows, :].astype(BF16)
        return carry

    lax.fori_loop(0, src_ref.shape[0] // rows_per_step, body, 0)


def _mm_kernel(*refs, n_pieces, cast_rows, transposed):
    a_refs, (b_ref, o_ref, bb_ref) = refs[:n_pieces], refs[n_pieces:]

    @pl.when(pl.program_id(1) == 0)
    def _():
        _cast_rows_to_bf16(b_ref, bb_ref, cast_rows)

    if transposed:
        (a_ref,) = a_refs
        acc = lax.dot_general(a_ref[...], bb_ref[...], (((1,), (1,)), ((), ())),
                              preferred_element_type=F32)
    else:
        acc, k0 = None, 0
        for a_ref in a_refs:
            k1 = k0 + a_ref.shape[1]
            part = jnp.dot(a_ref[...], bb_ref[k0:k1, :], preferred_element_type=F32)
            acc, k0 = part if acc is None else acc + part, k1
    o_ref[...] = acc.astype(o_ref.dtype)


def _matmul(a, w, layer, n_cols, out_dtype, *, transposed=False):
    pieces = a if isinstance(a, tuple) else (a,)
    m = pieces[0].shape[0]
    kdim = sum(p.shape[1] for p in pieces)
    tm = _pick_tile(m, (1024, 512, 256, 128))
    tn = _pick_tile(n_cols, (512, 384, 256, 128))
    if transposed:
        w_block, w_index, cast_rows = (None, tn, kdim), (lambda j, i: (layer, j, 0)), LANES
    else:
        w_block, w_index = (None, kdim, tn), (lambda j, i: (layer, 0, j))
        cast_rows = _pick_tile(kdim, (512, 256, 128))
    return pl.pallas_call(
        functools.partial(_mm_kernel, n_pieces=len(pieces), cast_rows=cast_rows,
                          transposed=transposed),
        out_shape=jax.ShapeDtypeStruct((m, n_cols), out_dtype),
        grid=(n_cols // tn, m // tm),
        in_specs=[pl.BlockSpec((tm, p.shape[1]), lambda j, i: (i, 0)) for p in pieces]
        + [pl.BlockSpec(w_block, w_index)],
        out_specs=pl.BlockSpec((tm, tn), lambda j, i: (i, j)),
        scratch_shapes=[pltpu.VMEM(w_block[1:], BF16)],
        compiler_params=_params(("arbitrary", "arbitrary")),
        name="matmul",
    )(*pieces, w)


def _gdn_kernel(q_ref, k_ref, v_ref, z_ref, wq_ref, wk_ref, wv_ref, a_ref, b_ref,
                alog_ref, dtb_ref, nw_ref, o_ref,
                s_ref, cq_ref, ck_ref, cv_ref, qs_ref, ks_ref, vs_ref, w_ref, qk_ref, dl_ref, ext_refs,
                *, tb, hb):
    c = GDN_CHUNK
    d = GDN_HEAD_DIM
    nc = tb // c

    @pl.when(pl.program_id(1) == 0)
    def _():
        s_ref[...] = jnp.zeros_like(s_ref)
        cq_ref[...] = jnp.zeros_like(cq_ref)
        ck_ref[...] = jnp.zeros_like(ck_ref)
        cv_ref[...] = jnp.zeros_like(cv_ref)

    def conv_body(ri, carry):
        r0 = pl.multiple_of(ri * c, c)
        rp = pl.multiple_of(jnp.maximum(r0 - SUBLANES, 0), SUBLANES)
        for a, (x_ref, c_ref, w_ref, dst_ref, scale) in enumerate(
                ((q_ref, cq_ref, wq_ref, qs_ref, d ** -0.5),
                 (k_ref, ck_ref, wk_ref, ks_ref, 1.0),
                 (v_ref, cv_ref, wv_ref, vs_ref, None))):
            for h in range(hb):
                sl = slice(h * d, (h + 1) * d)
                ext_ref = ext_refs.at[a * hb + h]
                ext_ref[0:SUBLANES, :] = jnp.where(ri == 0, c_ref[:, sl],
                                                   x_ref[pl.ds(rp, SUBLANES), sl])
                cur = x_ref[pl.ds(r0, c), sl]
                ext_ref[SUBLANES:, :] = cur
                w = w_ref[:, sl]
                acc = cur * w[CONV_WIDTH - 1:CONV_WIDTH, :]
                for j in range(1, CONV_WIDTH):
                    acc = acc + (ext_ref[SUBLANES - j:SUBLANES - j + c, :]
                                 * w[CONV_WIDTH - 1 - j:CONV_WIDTH - j, :])
                y = acc * _sigmoid(acc)
                if scale is not None:
                    y = y * (lax.rsqrt(jnp.sum(y * y, axis=-1, keepdims=True) + L2_EPS) * scale)
                dst_ref[pl.ds(r0, c), sl] = y
        return carry

    lax.fori_loop(0, nc, conv_body, 0)
    cq_ref[...] = q_ref[tb - SUBLANES:tb, :]
    ck_ref[...] = k_ref[tb - SUBLANES:tb, :]
    cv_ref[...] = v_ref[tb - SUBLANES:tb, :]

    ii = lax.broadcasted_iota(jnp.int32, (c, c), 0)
    jj = lax.broadcasted_iota(jnp.int32, (c, c), 1)
    causal = jj <= ii
    diag = jj == ii
    nw = nw_ref[...]

    heads = range(hb)
    sls = [slice(h * d, (h + 1) * d) for h in heads]

    ii2 = lax.broadcasted_iota(jnp.int32, (c, d), 0)
    ll2 = lax.broadcasted_iota(jnp.int32, (c, d), 1)
    left = ll2 < c
    jj2 = jnp.where(left, ll2, ll2 - c)
    causal2 = jj2 <= ii2
    strict2 = jj2 < ii2
    diag2 = jj2 == ii2
    eye2 = jnp.where(diag2, 1.0, 0.0).astype(F32)

    def block_diag(m):
        return jnp.concatenate([jnp.where(left, m, 0.0), jnp.where(left, 0.0, m)], axis=0)

    def side_by_side(a, b):
        z = jnp.zeros(a.shape, BF16)
        return jnp.concatenate([jnp.concatenate([a.astype(BF16), z], axis=1),
                                jnp.concatenate([z, b.astype(BF16)], axis=1)], axis=0)

    def solve_body(pi, carry):
        items = [(pi * SOLVE_CHUNKS + cj, h) for cj in range(SOLVE_CHUNKS) for h in heads]
        n = range(len(items))
        pairs = range(len(items) // 2)
        rows = [pl.ds(pl.multiple_of(ci * c, c), c) for ci, _ in items]
        qc = [qs_ref[rows[i], sls[h]] for i, (_, h) in enumerate(items)]
        kc = [ks_ref[rows[i], sls[h]] for i, (_, h) in enumerate(items)]
        vc = [vs_ref[rows[i], sls[h]] for i, (_, h) in enumerate(items)]
        decay_col, beta_col = [], []
        for ci, h in items:
            a_row = a_ref[h, pl.ds(ci, 1), :]
            b_row = b_ref[h, pl.ds(ci, 1), :]
            sp_in = a_row + dtb_ref[h]
            softplus = jnp.maximum(sp_in, 0.0) + jnp.log1p(jnp.exp(-jnp.abs(sp_in)))
            g_row = -jnp.exp(alog_ref[h]) * softplus
            beta_row = _sigmoid(b_row)
            g_b = jnp.broadcast_to(g_row, (c, c))
            decay_col.append(jnp.sum(jnp.where(causal, g_b, 0.0), axis=1, keepdims=True))
            beta_col.append(jnp.sum(jnp.where(diag, jnp.broadcast_to(beta_row, (c, c)), 0.0),
                                    axis=1, keepdims=True))
        gamma = []
        for j in pairs:
            dsel = jnp.where(left, decay_col[2 * j], decay_col[2 * j + 1])
            drow = jnp.sum(jnp.where(diag2, dsel, 0.0), axis=0, keepdims=True)
            gamma.append(jnp.where(causal2, jnp.exp(jnp.where(causal2, dsel - drow, 0.0)), 0.0))
        k_beta = [kc[i] * beta_col[i] for i in n]
        kq = [_bdot_nt(jnp.concatenate(
                  [jnp.concatenate([k_beta[2 * j].astype(BF16), k_beta[2 * j + 1].astype(BF16)], axis=1),
                   jnp.concatenate([qc[2 * j].astype(BF16), qc[2 * j + 1].astype(BF16)], axis=1)], axis=0),
                  side_by_side(kc[2 * j], kc[2 * j + 1])) for j in pairs]
        pw = [jnp.where(strict2, -(kq[j][:c, :] * gamma[j]), 0.0) for j in pairs]
        p = [eye2 + pw[j] for j in pairs]
        pw = [_bdot(pw[j], block_diag(pw[j])) for j in pairs]
        span = 2
        while 2 * span < c:
            st = [_bdot(jnp.concatenate([pw[j], p[j]], axis=0), block_diag(pw[j])) for j in pairs]
            p = [p[j] + st[j][c:, :] for j in pairs]
            pw = [st[j][:c, :] for j in pairs]
            span *= 2
        t_inv = [p[j] + _bdot(p[j], block_diag(pw[j])) for j in pairs]
        e_col = [jnp.exp(decay_col[i]) for i in n]
        rhs = [jnp.concatenate([(vc[i] * beta_col[i]).astype(BF16),
                                (k_beta[i] * e_col[i]).astype(BF16)], axis=1) for i in n]
        zero2 = jnp.zeros((c, 2 * d), BF16)
        uw = [_bdot(t_inv[j], jnp.concatenate(
                  [jnp.concatenate([rhs[2 * j], zero2], axis=1),
                   jnp.concatenate([zero2, rhs[2 * j + 1]], axis=1)], axis=0))
              for j in pairs]
        for i, (ci, h) in enumerate(items):
            j, off = i // 2, (i % 2) * 2 * d
            decay_last = decay_col[i][c - 1:c, :]
            vs_ref[rows[i], sls[h]] = uw[j][:, off:off + d]
            w_ref[rows[i], sls[h]] = uw[j][:, off + d:off + 2 * d]
            qs_ref[rows[i], sls[h]] = qc[i] * e_col[i]
            ks_ref[rows[i], sls[h]] = kc[i] * jnp.exp(decay_last - decay_col[i])
            dl_ref[ci * hb + h] = jnp.broadcast_to(jnp.exp(decay_last), (SUBLANES, d))
            if i % 2 == 0:
                qk_ref[h // 2, rows[i], :] = kq[j][c:, :] * gamma[j]
        return carry

    def state_body(ci, carry):
        rows = pl.ds(pl.multiple_of(ci * c, c), c)
        s = [s_ref[h] for h in heads]
        ws = [_bdot(jnp.concatenate([w_ref[rows, sls[h]], qs_ref[rows, sls[h]]], axis=0), s[h])
              for h in heads]
        v_new = [vs_ref[rows, sls[h]] - ws[h][:c, :] for h in heads]
        o_intra = [_bdot(qk_ref[j, rows, :], side_by_side(v_new[2 * j], v_new[2 * j + 1]))
                   for j in range(hb // 2)]
        kv = [_bdot_tn(ks_ref[rows, sls[h]], v_new[h]) for h in heads]
        for h in heads:
            s_ref[h] = s[h] * dl_ref[ci * hb + h][0:1, :] + kv[h]
            o = ws[h][c:, :] + o_intra[h // 2][:, (h % 2) * d:(h % 2 + 1) * d]
            o = o * lax.rsqrt(jnp.mean(o * o, axis=-1, keepdims=True) + RMS_EPS) * nw
            zc = z_ref[rows, sls[h]]
            o = o * (zc * _sigmoid(zc))
            o_ref[rows, sls[h]] = o.astype(o_ref.dtype)
        return carry

    lax.fori_loop(0, nc // SOLVE_CHUNKS, solve_body, 0)
    lax.fori_loop(0, nc, state_body, 0)


def _gdn(proj, conv_w, a_t, b_t, a_log, dt_bias, norm_w, *, heads):
    seq = proj.shape[0]
    d = GDN_HEAD_DIM
    hb = _pick_tile(heads, (8, 6, 4, 2))
    assert hb % 2 == 0 and d == 2 * GDN_CHUNK
    tb = _pick_tile(seq, (512,))
    n_hb = heads // hb
    wblk = hb * d
    kernel = functools.partial(_gdn_kernel, tb=tb, hb=hb)

    def col(offset):
        return lambda h, t: (t, offset * n_hb + h)

    def wcol(offset):
        return lambda h, t: (0, offset * n_hb + h)

    return pl.pallas_call(
        kernel,
        out_shape=jax.ShapeDtypeStruct((seq, heads * d), BF16),
        grid=(n_hb, seq // tb),
        in_specs=[pl.BlockSpec((tb, wblk), col(0)),
                  pl.BlockSpec((tb, wblk), col(1)),
                  pl.BlockSpec((tb, wblk), col(2)),
                  pl.BlockSpec((tb, wblk), col(3)),
                  pl.BlockSpec((CONV_WIDTH, wblk), wcol(0)),
                  pl.BlockSpec((CONV_WIDTH, wblk), wcol(1)),
                  pl.BlockSpec((CONV_WIDTH, wblk), wcol(2)),
                  pl.BlockSpec((hb, tb // GDN_CHUNK, GDN_CHUNK), lambda h, t: (h, t, 0)),
                  pl.BlockSpec((hb, tb // GDN_CHUNK, GDN_CHUNK), lambda h, t: (h, t, 0)),
                  pl.BlockSpec((hb, 1, 1), lambda h, t: (h, 0, 0)),
                  pl.BlockSpec((hb, 1, 1), lambda h, t: (h, 0, 0)),
                  pl.BlockSpec((1, d), lambda h, t: (0, 0))],
        out_specs=pl.BlockSpec((tb, wblk), lambda h, t: (t, h)),
        scratch_shapes=[pltpu.VMEM((hb, d, d), F32),
                        pltpu.VMEM((SUBLANES, wblk), F32),
                        pltpu.VMEM((SUBLANES, wblk), F32),
                        pltpu.VMEM((SUBLANES, wblk), F32),
                        pltpu.VMEM((tb, wblk), F32),
                        pltpu.VMEM((tb, wblk), F32),
                        pltpu.VMEM((tb, wblk), F32),
                        pltpu.VMEM((tb, wblk), F32),
                        pltpu.VMEM((hb // 2, tb, d), F32),
                        pltpu.VMEM((tb // GDN_CHUNK * hb, SUBLANES, d), F32),
                        pltpu.VMEM((3 * hb, SUBLANES + GDN_CHUNK, d), F32)],
        compiler_params=_params(("parallel", "arbitrary")),
        name="gdn",
    )(proj, proj, proj, proj, conv_w, conv_w, conv_w, a_t, b_t,
      a_log.reshape(heads, 1, 1), dt_bias.reshape(heads, 1, 1), norm_w.reshape(1, d))


def _gelu_tanh(y):
    return 0.5 * y * (1.0 + jnp.tanh(math.sqrt(2.0 / math.pi) * (y + 0.044715 * (y * y * y))))


def _s5_kernel(u_ref, bre_ref, bim_ref, cre_ref, cim_ref, are_ref, aim_ref, d_ref, wglu_ref,
               o_ref, xr_ref, xi_ref, str_ref, sti_ref, y_ref, wglu_bf_ref, *, tb, pitch, slabs):
    tiles = slabs * S5_TILES_PER_SLAB
    n_vreg = tiles // SUBLANES

    @pl.when(pl.program_id(0) == 0)
    def _():
        str_ref[...] = jnp.zeros_like(str_ref)
        sti_ref[...] = jnp.zeros_like(sti_ref)
        _cast_rows_to_bf16(wglu_ref, wglu_bf_ref, LANES)

    for s in range(slabs):
        ub = u_ref[:, s * LANES:(s + 1) * LANES].astype(BF16)
        r = jnp.dot(ub, bre_ref[s], preferred_element_type=F32)
        m = jnp.dot(ub, bim_ref[s], preferred_element_type=F32)
        for t4 in range(S5_TILES_PER_SLAB):
            lt = s * S5_TILES_PER_SLAB + t4
            xr_ref[lt * pitch:lt * pitch + tb, :] = r[:, t4 * LANES:(t4 + 1) * LANES]
            xi_ref[lt * pitch:lt * pitch + tb, :] = m[:, t4 * LANES:(t4 + 1) * LANES]

    ar = [are_ref[j] for j in range(n_vreg)]
    ai = [aim_ref[j] for j in range(n_vreg)]

    def step(t, carry):
        xr, xi = carry
        nxr, nxi = [], []
        for j in range(n_vreg):
            rows = pl.ds(j * SUBLANES * pitch + t, SUBLANES, stride=pitch)
            br = xr_ref[rows, :]
            bi = xi_ref[rows, :]
            r = ar[j] * xr[j] - ai[j] * xi[j] + br
            m = ar[j] * xi[j] + ai[j] * xr[j] + bi
            xr_ref[rows, :] = r
            xi_ref[rows, :] = m
            nxr.append(r)
            nxi.append(m)
        return tuple(nxr), tuple(nxi)

    x0 = (tuple(str_ref[j] for j in range(n_vreg)), tuple(sti_ref[j] for j in range(n_vreg)))
    xr_f, xi_f = lax.fori_loop(0, tb, step, x0, unroll=4)
    for j in range(n_vreg):
        str_ref[j] = xr_f[j]
        sti_ref[j] = xi_f[j]

    for s in range(slabs):
        acc = jnp.zeros((tb, LANES), F32)
        for t4 in range(S5_TILES_PER_SLAB):
            lt = s * S5_TILES_PER_SLAB + t4
            xr = xr_ref[lt * pitch:lt * pitch + tb, :].astype(BF16)
            xi = xi_ref[lt * pitch:lt * pitch + tb, :].astype(BF16)
            acc = acc + jnp.dot(xr, cre_ref[s, t4 * LANES:(t4 + 1) * LANES, :],
                                preferred_element_type=F32)
            acc = acc - jnp.dot(xi, cim_ref[s, t4 * LANES:(t4 + 1) * LANES, :],
                                preferred_element_type=F32)
        sl = slice(s * LANES, (s + 1) * LANES)
        y_ref[:, sl] = _gelu_tanh(acc + d_ref[:, sl] * u_ref[:, sl])

    y = y_ref[...]
    gate = jnp.dot(y.astype(BF16), wglu_bf_ref[...], preferred_element_type=F32)
    o_ref[...] = (y * _sigmoid(gate)).astype(o_ref.dtype)


def _s5(proj, u_col_block, lam_re, lam_im, b_re, b_im, c_re, c_im, d_skip, log_dt, w_glu):
    seq = proj.shape[0]
    groups = lam_re.shape[0]
    width = groups * S5_GROUP
    slabs = width // LANES
    tiles = slabs * S5_TILES_PER_SLAB
    assert tiles % SUBLANES == 0
    tb = _pick_tile(seq, (512,))
    pitch = tb + SUBLANES

    dt = jnp.exp(log_dt)[:, None]
    mag = jnp.exp(lam_re * dt)
    ab_re, ab_im = mag * jnp.cos(lam_im * dt), mag * jnp.sin(lam_im * dt)
    den = lam_re * lam_re + lam_im * lam_im
    nr, ni = ab_re - 1.0, ab_im
    coef_re = (nr * lam_re + ni * lam_im) / den
    coef_im = (ni * lam_re - nr * lam_im) / den
    bb_re = coef_re[..., None] * b_re - coef_im[..., None] * b_im
    bb_im = coef_re[..., None] * b_im + coef_im[..., None] * b_re
    eye = jnp.eye(S5_GROUPS_PER_SLAB, dtype=F32)

    def b_blockdiag(bb):
        t = bb.reshape(slabs, S5_GROUPS_PER_SLAB, S5_STATE, S5_GROUP)
        m = jnp.einsum('saph,ab->sahbp', t, eye)
        return m.reshape(slabs, LANES, S5_SLAB_STATES).astype(BF16)

    def c_blockdiag(cc):
        t = cc.reshape(slabs, S5_GROUPS_PER_SLAB, S5_GROUP, S5_STATE)
        m = jnp.einsum('sahp,ab->sapbh', t, eye)
        return m.reshape(slabs, S5_SLAB_STATES, LANES).astype(BF16)

    n_vreg = tiles // SUBLANES
    kernel = functools.partial(_s5_kernel, tb=tb, pitch=pitch, slabs=slabs)
    full3 = lambda i: (0, 0, 0)
    return pl.pallas_call(
        kernel,
        out_shape=jax.ShapeDtypeStruct((seq, width), BF16),
        grid=(seq // tb,),
        in_specs=[pl.BlockSpec((tb, width), lambda i: (i, u_col_block)),
                  pl.BlockSpec((slabs, LANES, S5_SLAB_STATES), full3),
                  pl.BlockSpec((slabs, LANES, S5_SLAB_STATES), full3),
                  pl.BlockSpec((slabs, S5_SLAB_STATES, LANES), full3),
                  pl.BlockSpec((slabs, S5_SLAB_STATES, LANES), full3),
                  pl.BlockSpec((n_vreg, SUBLANES, LANES), full3),
                  pl.BlockSpec((n_vreg, SUBLANES, LANES), full3),
                  pl.BlockSpec((1, width), lambda i: (0, 0)),
                  pl.BlockSpec((width, width), lambda i: (0, 0))],
        out_specs=pl.BlockSpec((tb, width), lambda i: (i, 0)),
        scratch_shapes=[pltpu.VMEM((tiles * pitch, LANES), F32),
                        pltpu.VMEM((tiles * pitch, LANES), F32),
                        pltpu.VMEM((n_vreg, SUBLANES, LANES), F32),
                        pltpu.VMEM((n_vreg, SUBLANES, LANES), F32),
                        pltpu.VMEM((tb, width), F32),
                        pltpu.VMEM((width, width), BF16)],
        compiler_params=_params(("arbitrary",)),
        name="s5",
    )(proj, b_blockdiag(bb_re), b_blockdiag(bb_im), c_blockdiag(c_re), c_blockdiag(c_im),
      ab_re.reshape(n_vreg, SUBLANES, LANES), ab_im.reshape(n_vreg, SUBLANES, LANES),
      d_skip.reshape(1, width), w_glu)


def _ln_rows(h, g_ref, b_ref):
    mu = jnp.mean(h, axis=-1, keepdims=True)
    hc = h - mu
    var = jnp.mean(hc * hc, axis=-1, keepdims=True)
    return hc * lax.rsqrt(var + LN_EPS) * g_ref[...] + b_ref[...]


def _pack_bf16_halves(y):
    half = y.shape[1] // 2
    bits = lax.bitcast_convert_type(y.astype(BF16).astype(F32), jnp.uint32)
    return (bits[:, :half] >> 16) | (bits[:, half:] & jnp.uint32(0xFFFF0000))


def _unpack_bf16_halves(p):
    lo = lax.bitcast_convert_type(p << 16, F32)
    hi = lax.bitcast_convert_type(p & jnp.uint32(0xFFFF0000), F32)
    return jnp.concatenate([lo, hi], axis=1)


def _ln_router_kernel(x_ref, m_ref, g_ref, b_ref, wr_ref, o_ref, op_ref, lg_ref, *, alpha):
    y = _ln_rows(alpha * x_ref[...] + m_ref[...], g_ref, b_ref)
    o_ref[...] = y
    op_ref[...] = _pack_bf16_halves(y)
    lg_ref[...] = jnp.dot(y, wr_ref[...], preferred_element_type=F32,
                          precision=lax.Precision.HIGHEST)


def _ln_pair_kernel(x_ref, p0_ref, p1_ref, tw_ref, g_ref, b_ref, o_ref, ob_ref, *, alpha):
    tw = tw_ref[...]
    ffn = (tw[:, 0:1] * _unpack_bf16_halves(p0_ref[...])
           + tw[:, 1:2] * _unpack_bf16_halves(p1_ref[...]))
    y = _ln_rows(alpha * x_ref[...] + ffn, g_ref, b_ref)
    o_ref[...] = y
    ob_ref[...] = y.astype(BF16)


def _residual_layer_norm(x, addends, g, b, alpha, w_router=None):
    seq, dm = x.shape
    tm = _pick_tile(seq, (256, 128))
    row = pl.BlockSpec((tm, dm), lambda i: (i, 0))
    half_row = pl.BlockSpec((tm, dm // 2), lambda i: (i, 0))
    vec = pl.BlockSpec((1, dm), lambda i: (0, 0))
    args = [x, *addends, g.reshape(1, dm), b.reshape(1, dm)]
    if w_router is None:
        assert len(addends) == 3
        kernel = functools.partial(_ln_pair_kernel, alpha=alpha)
        in_specs = [row, half_row, half_row, pl.BlockSpec((tm, TOP_K), lambda i: (i, 0)), vec, vec]
        out_shape = [jax.ShapeDtypeStruct((seq, dm), F32), jax.ShapeDtypeStruct((seq, dm), BF16)]
        out_specs = [row, row]
    else:
        assert len(addends) == 1
        kernel = functools.partial(_ln_router_kernel, alpha=alpha)
        nr = w_router.shape[1]
        args.append(w_router)
        in_specs = [row, row, vec, vec, pl.BlockSpec((dm, nr), lambda i: (0, 0))]
        out_shape = [jax.ShapeDtypeStruct((seq, dm), F32),
                     jax.ShapeDtypeStruct((seq, dm // 2), jnp.uint32),
                     jax.ShapeDtypeStruct((seq, nr), F32)]
        out_specs = [row, half_row, pl.BlockSpec((tm, nr), lambda i: (i, 0))]
    return pl.pallas_call(
        kernel,
        out_shape=out_shape,
        grid=(seq // tm,),
        in_specs=in_specs,
        out_specs=out_specs,
        compiler_params=_params(("parallel",)),
        name="residual_layer_norm",
    )(*args)


def _moe_kernel(be_ref, first_ref, nxt_ref, nused_ref, x_ref, wg_hbm, wu_hbm, wd_hbm, o_ref,
                sg_ref, su_ref, sd_ref, bg_ref, bu_ref, bd_ref, sem, *, layer):
    b = pl.program_id(0)

    def weight_copies(e):
        return (pltpu.make_async_copy(wg_hbm.at[layer, e], sg_ref, sem.at[0]),
                pltpu.make_async_copy(wu_hbm.at[layer, e], su_ref, sem.at[1]),
                pltpu.make_async_copy(wd_hbm.at[layer, e], sd_ref, sem.at[2]))

    @pl.when(b == 0)
    def _():
        for cp in weight_copies(be_ref[0]):
            cp.start()

    @pl.when(first_ref[b] == 1)
    def _():
        for cp in weight_copies(be_ref[b]):
            cp.wait()
        _cast_rows_to_bf16(sg_ref, bg_ref, 512)
        _cast_rows_to_bf16(su_ref, bu_ref, 512)
        _cast_rows_to_bf16(sd_ref, bd_ref, 64)

        @pl.when(nxt_ref[b] >= 0)
        def _():
            for cp in weight_copies(nxt_ref[b]):
                cp.start()

    @pl.when(b >= nused_ref[0])
    def _():
        o_ref[...] = jnp.zeros_like(o_ref)

    @pl.when(b < nused_ref[0])
    def _():
        x = _unpack_bf16_halves(x_ref[...]).astype(BF16)
        g = jnp.dot(x, bg_ref[...], preferred_element_type=F32)
        u = jnp.dot(x, bu_ref[...], preferred_element_type=F32)
        hid = (g * _sigmoid(g)) * u
        y = jnp.dot(hid.astype(BF16), bd_ref[...], preferred_element_type=F32)
        o_ref[...] = _pack_bf16_halves(y)


def _moe_experts(xp, blk_expert, blk_first, blk_next, n_used, w_gate, w_up, w_down, layer, *, bm):
    n_rows = xp.shape[0]
    dm, de = w_gate.shape[2:]
    n_blocks = n_rows // bm
    assert dm % 512 == 0 and de % 64 == 0
    grid_spec = pltpu.PrefetchScalarGridSpec(
        num_scalar_prefetch=4,
        grid=(n_blocks,),
        in_specs=[pl.BlockSpec((bm, dm // 2), lambda i, be, fi, nx, nu: (i, 0)),
                  pl.BlockSpec(memory_space=pl.ANY),
                  pl.BlockSpec(memory_space=pl.ANY),
                  pl.BlockSpec(memory_space=pl.ANY)],
        out_specs=pl.BlockSpec((bm, dm // 2), lambda i, be, fi, nx, nu: (i, 0)),
        scratch_shapes=[pltpu.VMEM((dm, de), F32), pltpu.VMEM((dm, de), F32),
                        pltpu.VMEM((de, dm), F32),
                        pltpu.VMEM((dm, de), BF16), pltpu.VMEM((dm, de), BF16),
                        pltpu.VMEM((de, dm), BF16),
                        pltpu.SemaphoreType.DMA((3,))],
    )
    return pl.pallas_call(
        functools.partial(_moe_kernel, layer=layer),
        out_shape=jax.ShapeDtypeStruct((n_rows, dm // 2), jnp.uint32),
        grid_spec=grid_spec,
        compiler_params=_params(("arbitrary",)),
        name="moe_experts",
    )(blk_expert, blk_first, blk_next, n_used, xp, w_gate, w_up, w_down)


def _moe(xp, logits, b_rg, b_re, w_gate, w_up, w_down, layer, *, bm):
    n_tok = xp.shape[0]
    n_experts = N_GROUPS * EXPERTS_PER_GROUP
    grp_logits = logits[:, :N_GROUPS] + b_rg
    exp_logits = logits[:, N_GROUPS:N_GROUPS + n_experts].reshape(n_tok, N_GROUPS, EXPERTS_PER_GROUP) + b_re
    grp_probs = jax.nn.softmax(grp_logits, axis=-1)
    grp_idx = jnp.argmax(grp_probs, axis=-1).astype(jnp.int32)[:, None]
    grp_p = jnp.max(grp_probs, axis=-1, keepdims=True)
    sel = exp_logits[:, 0]
    for g in range(1, N_GROUPS):
        sel = jnp.where(grp_idx == g, exp_logits[:, g], sel)
    lane = jnp.arange(EXPERTS_PER_GROUP, dtype=jnp.int32)[None, :]
    idx1 = jnp.argmax(sel, axis=-1).astype(jnp.int32)[:, None]
    rest = jnp.where(lane == idx1, -jnp.inf, sel)
    idx2 = jnp.argmax(rest, axis=-1).astype(jnp.int32)[:, None]
    top_idx = jnp.concatenate([idx1, idx2], axis=1)
    top_logit = jnp.concatenate([jnp.max(sel, axis=-1, keepdims=True),
                                 jnp.max(rest, axis=-1, keepdims=True)], axis=1)
    top_w = jax.nn.softmax(top_logit, axis=-1) * grp_p
    expert_id = (grp_idx * EXPERTS_PER_GROUP + top_idx).reshape(-1).astype(jnp.int32)
    n_assign = n_tok * TOP_K
    onehot = (expert_id[:, None] == jnp.arange(n_experts, dtype=jnp.int32)[None, :]).astype(jnp.int32)
    csum = jnp.cumsum(onehot, axis=0)
    counts = csum[-1]
    padded = (counts + bm - 1) // bm * bm
    pend = jnp.cumsum(padded)
    pstart = pend - padded
    dest = jnp.sum(onehot * (pstart[None, :] + csum - 1), axis=1)
    n_blocks = (n_assign + bm - 1) // bm + n_experts
    n_rows = n_blocks * bm
    blk_start = jnp.arange(n_blocks, dtype=jnp.int32) * bm
    blk_expert = jnp.minimum(jnp.searchsorted(pend, blk_start, side='right'),
                             n_experts - 1).astype(jnp.int32)
    row_asg = jnp.full((n_rows,), -1, jnp.int32).at[dest].set(jnp.arange(n_assign, dtype=jnp.int32))
    row_tok = jnp.where(row_asg >= 0, row_asg // TOP_K,
                        jnp.arange(n_rows, dtype=jnp.int32) % n_tok)
    blk_first = jnp.concatenate([jnp.ones((1,), jnp.int32),
                                 (blk_expert[1:] != blk_expert[:-1]).astype(jnp.int32)])
    nxt_idx = jnp.searchsorted(blk_expert, blk_expert, side='right')
    blk_next = jnp.where(nxt_idx < n_blocks, blk_expert[jnp.minimum(nxt_idx, n_blocks - 1)],
                         -1).astype(jnp.int32)
    xg = xp[row_tok]
    n_used = (pend[-1:] // bm).astype(jnp.int32)
    yb = _moe_experts(xg, blk_expert, blk_first, blk_next, n_used, w_gate, w_up, w_down, layer, bm=bm)
    pos = dest.reshape(n_tok, TOP_K)
    return yb[pos[:, 0]], yb[pos[:, 1]], top_w


def kernel(x, w_in, gdn_conv_w, gdn_a_log, gdn_dt_bias, gdn_norm_w, s5_lambda_re, s5_lambda_im, s5_b_re, s5_b_im, s5_c_re, s5_c_im, s5_d, s5_log_dt, s5_w_glu, w_out, ln1_g, ln1_b, router_group_w, router_group_b, router_expert_w, router_expert_b, expert_w_gate, expert_w_up, expert_w_down, ln2_g, ln2_b):
    bsz, seq, dm = x.shape
    assert bsz == 1
    depth = w_in.shape[0]
    heads = gdn_a_log.shape[1]
    gdn_width = heads * GDN_HEAD_DIM
    s5_width = s5_d.shape[1]
    c_z = 4 * gdn_width
    c_b = c_z + 2 * heads
    alpha = (2 * depth) ** 0.25
    n_experts = N_GROUPS * EXPERTS_PER_GROUP
    router_cols = LANES
    moe_bm = 256
    tail_cols = s5_width + LANES

    w_in_t = jnp.swapaxes(w_in, 1, 2)
    xf = x.reshape(seq, dm)
    xb = xf.astype(BF16)
    for i in range(depth):
        w_rest = lax.slice(w_in_t, (i, c_z, 0), (i + 1, w_in_t.shape[1], dm))[0]
        w_tail = jnp.concatenate(
            [w_rest[2 * heads:], w_rest[:2 * heads], jnp.zeros((LANES - 2 * heads, dm), F32)], axis=0)
        proj = _matmul(xb, w_in_t, i, c_z, F32, transposed=True)
        proj_tail = _matmul(xb, w_tail[None], 0, tail_cols, F32, transposed=True)
        a_t = proj_tail[:, s5_width:s5_width + heads].T.reshape(heads, seq // GDN_CHUNK, GDN_CHUNK)
        b_t = proj_tail[:, s5_width + heads:s5_width + 2 * heads].T.reshape(
            heads, seq // GDN_CHUNK, GDN_CHUNK)
        y_gdn = _gdn(proj, gdn_conv_w[i], a_t, b_t, gdn_a_log[i], gdn_dt_bias[i], gdn_norm_w[i],
                     heads=heads)
        y_s5 = _s5(proj_tail, 0, s5_lambda_re[i], s5_lambda_im[i], s5_b_re[i], s5_b_im[i],
                   s5_c_re[i], s5_c_im[i], s5_d[i], s5_log_dt[i], s5_w_glu[i])
        mix = _matmul((y_gdn, y_s5), w_out, i, dm, F32)
        w_router = jnp.concatenate(
            [router_group_w[i],
             jnp.transpose(router_expert_w[i], (1, 0, 2)).reshape(dm, n_experts),
             jnp.zeros((dm, router_cols - N_GROUPS - n_experts), F32)], axis=1)
        xf, xp, logits = _residual_layer_norm(xf, (mix,), ln1_g[i], ln1_b[i], alpha, w_router)
        ffn_pair = _moe(xp, logits, router_group_b[i], router_expert_b[i],
                        expert_w_gate, expert_w_up, expert_w_down, i, bm=moe_bm)
        xf, xb = _residual_layer_norm(xf, ffn_pair, ln2_g[i], ln2_b[i], alpha)
    return xf.reshape(bsz, seq, dm)
```

```python
import functools
import math

import jax
import jax.numpy as jnp
from jax import lax
from jax.experimental import pallas as pl
from jax.experimental.pallas import tpu as pltpu

F32 = jnp.float32
BF16 = jnp.bfloat16

LANES = 128
SUBLANES = 8
VMEM_LIMIT_BYTES = 56 * 1024 * 1024

GDN_CHUNK = 64
GDN_HEAD_DIM = 128
SOLVE_CHUNKS = 8
CONV_WIDTH = 4
S5_GROUP = 16
S5_STATE = 64
S5_GROUPS_PER_SLAB = LANES // S5_GROUP
S5_SLAB_STATES = S5_GROUPS_PER_SLAB * S5_STATE
S5_TILES_PER_SLAB = S5_SLAB_STATES // LANES
N_GROUPS = 4
EXPERTS_PER_GROUP = 8
TOP_K = 2
LN_EPS = 1e-5
RMS_EPS = 1e-6
L2_EPS = 1e-6


def _params(sem):
    return pltpu.CompilerParams(dimension_semantics=sem, vmem_limit_bytes=VMEM_LIMIT_BYTES)


def _pick_tile(n, candidates):
    for c in candidates:
        if n % c == 0:
            return c
    return n


def _bdot(a, b):
    return jnp.dot(a.astype(BF16), b.astype(BF16), preferred_element_type=F32)


def _bdot_nt(a, b):
    return lax.dot_general(a.astype(BF16), b.astype(BF16), (((1,), (1,)), ((), ())),
                           preferred_element_type=F32)


def _bdot_tn(a, b):
    return lax.dot_general(a.astype(BF16), b.astype(BF16), (((0,), (0,)), ((), ())),
                           preferred_element_type=F32)


def _sigmoid(x):
    return 1.0 / (1.0 + jnp.exp(-x))


def _cast_rows_to_bf16(src_ref, dst_ref, rows_per_step):
    def body(r, carry):
        rows = pl.ds(pl.multiple_of(r * rows_per_step, rows_per_step), rows_per_step)
        dst_ref[rows, :] = src_ref[rows, :].astype(BF16)
        return carry

    lax.fori_loop(0, src_ref.shape[0] // rows_per_step, body, 0)


def _mm_kernel(*refs, n_pieces, cast_rows, transposed):
    a_refs, (b_ref, o_ref, bb_ref) = refs[:n_pieces], refs[n_pieces:]

    @pl.when(pl.program_id(1) == 0)
    def _():
        _cast_rows_to_bf16(b_ref, bb_ref, cast_rows)

    if transposed:
        (a_ref,) = a_refs
        acc = lax.dot_general(a_ref[...], bb_ref[...], (((1,), (1,)), ((), ())),
                              preferred_element_type=F32)
    else:
        acc, k0 = None, 0
        for a_ref in a_refs:
            k1 = k0 + a_ref.shape[1]
            part = jnp.dot(a_ref[...], bb_ref[k0:k1, :], preferred_element_type=F32)
            acc, k0 = part if acc is None else acc + part, k1
    o_ref[...] = acc.astype(o_ref.dtype)


def _matmul(a, w, layer, n_cols, out_dtype, *, transposed=False):
    pieces = a if isinstance(a, tuple) else (a,)
    m = pieces[0].shape[0]
    kdim = sum(p.shape[1] for p in pieces)
    tm = _pick_tile(m, (512, 256, 128))
    tn = _pick_tile(n_cols, (1024, 512, 384, 256, 128))
    if transposed:
        w_block, w_index, cast_rows = (None, tn, kdim), (lambda j, i: (layer, j, 0)), LANES
    else:
        w_block, w_index = (None, kdim, tn), (lambda j, i: (layer, 0, j))
        cast_rows = _pick_tile(kdim, (512, 256, 128))
    return pl.pallas_call(
        functools.partial(_mm_kernel, n_pieces=len(pieces), cast_rows=cast_rows,
                          transposed=transposed),
        out_shape=jax.ShapeDtypeStruct((m, n_cols), out_dtype),
        grid=(n_cols // tn, m // tm),
        in_specs=[pl.BlockSpec((tm, p.shape[1]), lambda j, i: (i, 0)) for p in pieces]
        + [pl.BlockSpec(w_block, w_index)],
        out_specs=pl.BlockSpec((tm, tn), lambda j, i: (i, j)),
        scratch_shapes=[pltpu.VMEM(w_block[1:], BF16)],
        compiler_params=_params(("arbitrary", "arbitrary")),
        name="matmul",
    )(*pieces, w)


def _gdn_kernel(q_ref, k_ref, v_ref, z_ref, wq_ref, wk_ref, wv_ref, a_ref, b_ref,
                alog_ref, dtb_ref, nw_ref, o_ref,
                s_ref, cq_ref, ck_ref, cv_ref, qs_ref, ks_ref, vs_ref, w_ref, qk_ref, dl_ref, ext_refs,
                *, tb, hb):
    c = GDN_CHUNK
    d = GDN_HEAD_DIM
    nc = tb // c

    @pl.when(pl.program_id(1) == 0)
    def _():
        s_ref[...] = jnp.zeros_like(s_ref)
        cq_ref[...] = jnp.zeros_like(cq_ref)
        ck_ref[...] = jnp.zeros_like(ck_ref)
        cv_ref[...] = jnp.zeros_like(cv_ref)

    def conv_body(ri, carry):
        r0 = pl.multiple_of(ri * c, c)
        rp = pl.multiple_of(jnp.maximum(r0 - SUBLANES, 0), SUBLANES)
        for a, (x_ref, c_ref, w_ref, dst_ref, scale) in enumerate(
                ((q_ref, cq_ref, wq_ref, qs_ref, d ** -0.5),
                 (k_ref, ck_ref, wk_ref, ks_ref, 1.0),
                 (v_ref, cv_ref, wv_ref, vs_ref, None))):
            for h in range(hb):
                sl = slice(h * d, (h + 1) * d)
                ext_ref = ext_refs.at[a * hb + h]
                ext_ref[0:SUBLANES, :] = jnp.where(ri == 0, c_ref[:, sl],
                                                   x_ref[pl.ds(rp, SUBLANES), sl])
                cur = x_ref[pl.ds(r0, c), sl]
                ext_ref[SUBLANES:, :] = cur
                w = w_ref[:, sl]
                acc = cur * w[CONV_WIDTH - 1:CONV_WIDTH, :]
                for j in range(1, CONV_WIDTH):
                    acc = acc + (ext_ref[SUBLANES - j:SUBLANES - j + c, :]
                                 * w[CONV_WIDTH - 1 - j:CONV_WIDTH - j, :])
                y = acc * _sigmoid(acc)
                if scale is not None:
                    y = y * (lax.rsqrt(jnp.sum(y * y, axis=-1, keepdims=True) + L2_EPS) * scale)
                dst_ref[pl.ds(r0, c), sl] = y
        return carry

    lax.fori_loop(0, nc, conv_body, 0)
    cq_ref[...] = q_ref[tb - SUBLANES:tb, :]
    ck_ref[...] = k_ref[tb - SUBLANES:tb, :]
    cv_ref[...] = v_ref[tb - SUBLANES:tb, :]

    ii = lax.broadcasted_iota(jnp.int32, (c, c), 0)
    jj = lax.broadcasted_iota(jnp.int32, (c, c), 1)
    causal = jj <= ii
    diag = jj == ii
    nw = nw_ref[...]

    heads = range(hb)
    sls = [slice(h * d, (h + 1) * d) for h in heads]

    ii2 = lax.broadcasted_iota(jnp.int32, (c, d), 0)
    ll2 = lax.broadcasted_iota(jnp.int32, (c, d), 1)
    left = ll2 < c
    jj2 = jnp.where(left, ll2, ll2 - c)
    causal2 = jj2 <= ii2
    strict2 = jj2 < ii2
    diag2 = jj2 == ii2
    eye2 = jnp.where(diag2, 1.0, 0.0).astype(F32)

    def block_diag(m):
        return jnp.concatenate([jnp.where(left, m, 0.0), jnp.where(left, 0.0, m)], axis=0)

    def side_by_side(a, b):
        z = jnp.zeros(a.shape, BF16)
        return jnp.concatenate([jnp.concatenate([a.astype(BF16), z], axis=1),
                                jnp.concatenate([z, b.astype(BF16)], axis=1)], axis=0)

    def solve_body(pi, carry):
        items = [(pi * SOLVE_CHUNKS + cj, h) for cj in range(SOLVE_CHUNKS) for h in heads]
        n = range(len(items))
        pairs = range(len(items) // 2)
        rows = [pl.ds(pl.multiple_of(ci * c, c), c) for ci, _ in items]
        qc = [qs_ref[rows[i], sls[h]] for i, (_, h) in enumerate(items)]
        kc = [ks_ref[rows[i], sls[h]] for i, (_, h) in enumerate(items)]
        vc = [vs_ref[rows[i], sls[h]] for i, (_, h) in enumerate(items)]
        decay_col, beta_col = [], []
        for ci, h in items:
            a_row = a_ref[h, pl.ds(ci, 1), :]
            b_row = b_ref[h, pl.ds(ci, 1), :]
            sp_in = a_row + dtb_ref[h]
            softplus = jnp.maximum(sp_in, 0.0) + jnp.log1p(jnp.exp(-jnp.abs(sp_in)))
            g_row = -jnp.exp(alog_ref[h]) * softplus
            beta_row = _sigmoid(b_row)
            g_b = jnp.broadcast_to(g_row, (c, c))
            decay_col.append(jnp.sum(jnp.where(causal, g_b, 0.0), axis=1, keepdims=True))
            beta_col.append(jnp.sum(jnp.where(diag, jnp.broadcast_to(beta_row, (c, c)), 0.0),
                                    axis=1, keepdims=True))
        gamma = []
        for j in pairs:
            dsel = jnp.where(left, decay_col[2 * j], decay_col[2 * j + 1])
            drow = jnp.sum(jnp.where(diag2, dsel, 0.0), axis=0, keepdims=True)
            gamma.append(jnp.where(causal2, jnp.exp(jnp.where(causal2, dsel - drow, 0.0)), 0.0))
        k_beta = [kc[i] * beta_col[i] for i in n]
        kq = [_bdot_nt(jnp.concatenate(
                  [jnp.concatenate([k_beta[2 * j].astype(BF16), k_beta[2 * j + 1].astype(BF16)], axis=1),
                   jnp.concatenate([qc[2 * j].astype(BF16), qc[2 * j + 1].astype(BF16)], axis=1)], axis=0),
                  side_by_side(kc[2 * j], kc[2 * j + 1])) for j in pairs]
        pw = [jnp.where(strict2, -(kq[j][:c, :] * gamma[j]), 0.0) for j in pairs]
        p = [eye2 + pw[j] for j in pairs]
        pw = [_bdot(pw[j], block_diag(pw[j])) for j in pairs]
        span = 2
        while 2 * span < c:
            st = [_bdot(jnp.concatenate([pw[j], p[j]], axis=0), block_diag(pw[j])) for j in pairs]
            p = [p[j] + st[j][c:, :] for j in pairs]
            pw = [st[j][:c, :] for j in pairs]
            span *= 2
        t_inv = [p[j] + _bdot(p[j], block_diag(pw[j])) for j in pairs]
        e_col = [jnp.exp(decay_col[i]) for i in n]
        rhs = [jnp.concatenate([(vc[i] * beta_col[i]).astype(BF16),
                                (k_beta[i] * e_col[i]).astype(BF16)], axis=1) for i in n]
        zero2 = jnp.zeros((c, 2 * d), BF16)
        uw = [_bdot(t_inv[j], jnp.concatenate(
                  [jnp.concatenate([rhs[2 * j], zero2], axis=1),
                   jnp.concatenate([zero2, rhs[2 * j + 1]], axis=1)], axis=0))
              for j in pairs]
        for i, (ci, h) in enumerate(items):
            j, off = i // 2, (i % 2) * 2 * d
            decay_last = decay_col[i][c - 1:c, :]
            vs_ref[rows[i], sls[h]] = uw[j][:, off:off + d]
            w_ref[rows[i], sls[h]] = uw[j][:, off + d:off + 2 * d]
            qs_ref[rows[i], sls[h]] = qc[i] * e_col[i]
            ks_ref[rows[i], sls[h]] = kc[i] * jnp.exp(decay_last - decay_col[i])
            dl_ref[ci * hb + h] = jnp.broadcast_to(jnp.exp(decay_last), (SUBLANES, d))
            if i % 2 == 0:
                qk_ref[h // 2, rows[i], :] = kq[j][c:, :] * gamma[j]
        return carry

    def state_body(ci, carry):
        rows = pl.ds(pl.multiple_of(ci * c, c), c)
        s = [s_ref[h] for h in heads]
        ws = [_bdot(jnp.concatenate([w_ref[rows, sls[h]], qs_ref[rows, sls[h]]], axis=0), s[h])
              for h in heads]
        v_new = [vs_ref[rows, sls[h]] - ws[h][:c, :] for h in heads]
        o_intra = [_bdot(qk_ref[j, rows, :], side_by_side(v_new[2 * j], v_new[2 * j + 1]))
                   for j in range(hb // 2)]
        kv = [_bdot_tn(ks_ref[rows, sls[h]], v_new[h]) for h in heads]
        for h in heads:
            s_ref[h] = s[h] * dl_ref[ci * hb + h][0:1, :] + kv[h]
            o = ws[h][c:, :] + o_intra[h // 2][:, (h % 2) * d:(h % 2 + 1) * d]
            o = o * lax.rsqrt(jnp.mean(o * o, axis=-1, keepdims=True) + RMS_EPS) * nw
            zc = z_ref[rows, sls[h]]
            o = o * (zc * _sigmoid(zc))
            o_ref[rows, sls[h]] = o.astype(o_ref.dtype)
        return carry

    lax.fori_loop(0, nc // SOLVE_CHUNKS, solve_body, 0)
    lax.fori_loop(0, nc, state_body, 0)


def _gdn(proj, conv_w, a_t, b_t, a_log, dt_bias, norm_w, *, heads):
    seq = proj.shape[0]
    d = GDN_HEAD_DIM
    hb = _pick_tile(heads, (8, 6, 4, 2))
    assert hb % 2 == 0 and d == 2 * GDN_CHUNK
    tb = _pick_tile(seq, (512,))
    n_hb = heads // hb
    wblk = hb * d
    kernel = functools.partial(_gdn_kernel, tb=tb, hb=hb)

    def col(offset):
        return lambda h, t: (t, offset * n_hb + h)

    def wcol(offset):
        return lambda h, t: (0, offset * n_hb + h)

    return pl.pallas_call(
        kernel,
        out_shape=jax.ShapeDtypeStruct((seq, heads * d), BF16),
        grid=(n_hb, seq // tb),
        in_specs=[pl.BlockSpec((tb, wblk), col(0)),
                  pl.BlockSpec((tb, wblk), col(1)),
                  pl.BlockSpec((tb, wblk), col(2)),
                  pl.BlockSpec((tb, wblk), col(3)),
                  pl.BlockSpec((CONV_WIDTH, wblk), wcol(0)),
                  pl.BlockSpec((CONV_WIDTH, wblk), wcol(1)),
                  pl.BlockSpec((CONV_WIDTH, wblk), wcol(2)),
                  pl.BlockSpec((hb, tb // GDN_CHUNK, GDN_CHUNK), lambda h, t: (h, t, 0)),
                  pl.BlockSpec((hb, tb // GDN_CHUNK, GDN_CHUNK), lambda h, t: (h, t, 0)),
                  pl.BlockSpec((hb, 1, 1), lambda h, t: (h, 0, 0)),
                  pl.BlockSpec((hb, 1, 1), lambda h, t: (h, 0, 0)),
                  pl.BlockSpec((1, d), lambda h, t: (0, 0))],
        out_specs=pl.BlockSpec((tb, wblk), lambda h, t: (t, h)),
        scratch_shapes=[pltpu.VMEM((hb, d, d), F32),
                        pltpu.VMEM((SUBLANES, wblk), F32),
                        pltpu.VMEM((SUBLANES, wblk), F32),
                        pltpu.VMEM((SUBLANES, wblk), F32),
                        pltpu.VMEM((tb, wblk), F32),
                        pltpu.VMEM((tb, wblk), F32),
                        pltpu.VMEM((tb, wblk), F32),
                        pltpu.VMEM((tb, wblk), F32),
                        pltpu.VMEM((hb // 2, tb, d), F32),
                        pltpu.VMEM((tb // GDN_CHUNK * hb, SUBLANES, d), F32),
                        pltpu.VMEM((3 * hb, SUBLANES + GDN_CHUNK, d), F32)],
        compiler_params=_params(("parallel", "arbitrary")),
        name="gdn",
    )(proj, proj, proj, proj, conv_w, conv_w, conv_w, a_t, b_t,
      a_log.reshape(heads, 1, 1), dt_bias.reshape(heads, 1, 1), norm_w.reshape(1, d))


def _gelu_tanh(y):
    return 0.5 * y * (1.0 + jnp.tanh(math.sqrt(2.0 / math.pi) * (y + 0.044715 * (y * y * y))))


def _s5_kernel(u_ref, bre_ref, bim_ref, cre_ref, cim_ref, are_ref, aim_ref, d_ref, wglu_ref,
               o_ref, xr_ref, xi_ref, str_ref, sti_ref, y_ref, wglu_bf_ref, *, tb, pitch, slabs):
    tiles = slabs * S5_TILES_PER_SLAB
    n_vreg = tiles // SUBLANES

    @pl.when(pl.program_id(0) == 0)
    def _():
        str_ref[...] = jnp.zeros_like(str_ref)
        sti_ref[...] = jnp.zeros_like(sti_ref)
        _cast_rows_to_bf16(wglu_ref, wglu_bf_ref, LANES)

    for s in range(slabs):
        ub = u_ref[:, s * LANES:(s + 1) * LANES].astype(BF16)
        r = jnp.dot(ub, bre_ref[s], preferred_element_type=F32)
        m = jnp.dot(ub, bim_ref[s], preferred_element_type=F32)
        for t4 in range(S5_TILES_PER_SLAB):
            lt = s * S5_TILES_PER_SLAB + t4
            xr_ref[lt * pitch:lt * pitch + tb, :] = r[:, t4 * LANES:(t4 + 1) * LANES]
            xi_ref[lt * pitch:lt * pitch + tb, :] = m[:, t4 * LANES:(t4 + 1) * LANES]

    ar = [are_ref[j] for j in range(n_vreg)]
    ai = [aim_ref[j] for j in range(n_vreg)]

    def step(t, carry):
        xr, xi = carry
        nxr, nxi = [], []
        for j in range(n_vreg):
            rows = pl.ds(j * SUBLANES * pitch + t, SUBLANES, stride=pitch)
            br = xr_ref[rows, :]
            bi = xi_ref[rows, :]
            r = ar[j] * xr[j] - ai[j] * xi[j] + br
            m = ar[j] * xi[j] + ai[j] * xr[j] + bi
            xr_ref[rows, :] = r
            xi_ref[rows, :] = m
            nxr.append(r)
            nxi.append(m)
        return tuple(nxr), tuple(nxi)

    x0 = (tuple(str_ref[j] for j in range(n_vreg)), tuple(sti_ref[j] for j in range(n_vreg)))
    xr_f, xi_f = lax.fori_loop(0, tb, step, x0, unroll=8)
    for j in range(n_vreg):
        str_ref[j] = xr_f[j]
        sti_ref[j] = xi_f[j]

    for s in range(slabs):
        acc = jnp.zeros((tb, LANES), F32)
        for t4 in range(S5_TILES_PER_SLAB):
            lt = s * S5_TILES_PER_SLAB + t4
            xr = xr_ref[lt * pitch:lt * pitch + tb, :].astype(BF16)
            xi = xi_ref[lt * pitch:lt * pitch + tb, :].astype(BF16)
            acc = acc + jnp.dot(xr, cre_ref[s, t4 * LANES:(t4 + 1) * LANES, :],
                                preferred_element_type=F32)
            acc = acc - jnp.dot(xi, cim_ref[s, t4 * LANES:(t4 + 1) * LANES, :],
                                preferred_element_type=F32)
        sl = slice(s * LANES, (s + 1) * LANES)
        y_ref[:, sl] = _gelu_tanh(acc + d_ref[:, sl] * u_ref[:, sl])

    y = y_ref[...]
    gate = jnp.dot(y.astype(BF16), wglu_bf_ref[...], preferred_element_type=F32)
    o_ref[...] = (y * _sigmoid(gate)).astype(o_ref.dtype)


def _s5(proj, u_col_block, lam_re, lam_im, b_re, b_im, c_re, c_im, d_skip, log_dt, w_glu):
    seq = proj.shape[0]
    groups = lam_re.shape[0]
    width = groups * S5_GROUP
    slabs = width // LANES
    tiles = slabs * S5_TILES_PER_SLAB
    assert tiles % SUBLANES == 0
    tb = _pick_tile(seq, (512,))
    pitch = tb + SUBLANES

    dt = jnp.exp(log_dt)[:, None]
    mag = jnp.exp(lam_re * dt)
    ab_re, ab_im = mag * jnp.cos(lam_im * dt), mag * jnp.sin(lam_im * dt)
    den = lam_re * lam_re + lam_im * lam_im
    nr, ni = ab_re - 1.0, ab_im
    coef_re = (nr * lam_re + ni * lam_im) / den
    coef_im = (ni * lam_re - nr * lam_im) / den
    bb_re = coef_re[..., None] * b_re - coef_im[..., None] * b_im
    bb_im = coef_re[..., None] * b_im + coef_im[..., None] * b_re
    eye = jnp.eye(S5_GROUPS_PER_SLAB, dtype=F32)

    def b_blockdiag(bb):
        t = bb.reshape(slabs, S5_GROUPS_PER_SLAB, S5_STATE, S5_GROUP)
        m = jnp.einsum('saph,ab->sahbp', t, eye)
        return m.reshape(slabs, LANES, S5_SLAB_STATES).astype(BF16)

    def c_blockdiag(cc):
        t = cc.reshape(slabs, S5_GROUPS_PER_SLAB, S5_GROUP, S5_STATE)
        m = jnp.einsum('sahp,ab->sapbh', t, eye)
        return m.reshape(slabs, S5_SLAB_STATES, LANES).astype(BF16)

    n_vreg = tiles // SUBLANES
    kernel = functools.partial(_s5_kernel, tb=tb, pitch=pitch, slabs=slabs)
    full3 = lambda i: (0, 0, 0)
    return pl.pallas_call(
        kernel,
        out_shape=jax.ShapeDtypeStruct((seq, width), BF16),
        grid=(seq // tb,),
        in_specs=[pl.BlockSpec((tb, width), lambda i: (i, u_col_block)),
                  pl.BlockSpec((slabs, LANES, S5_SLAB_STATES), full3),
                  pl.BlockSpec((slabs, LANES, S5_SLAB_STATES), full3),
                  pl.BlockSpec((slabs, S5_SLAB_STATES, LANES), full3),
                  pl.BlockSpec((slabs, S5_SLAB_STATES, LANES), full3),
                  pl.BlockSpec((n_vreg, SUBLANES, LANES), full3),
                  pl.BlockSpec((n_vreg, SUBLANES, LANES), full3),
                  pl.BlockSpec((1, width), lambda i: (0, 0)),
                  pl.BlockSpec((width, width), lambda i: (0, 0))],
        out_specs=pl.BlockSpec((tb, width), lambda i: (i, 0)),
        scratch_shapes=[pltpu.VMEM((tiles * pitch, LANES), F32),
                        pltpu.VMEM((tiles * pitch, LANES), F32),
                        pltpu.VMEM((n_vreg, SUBLANES, LANES), F32),
                        pltpu.VMEM((n_vreg, SUBLANES, LANES), F32),
                        pltpu.VMEM((tb, width), F32),
                        pltpu.VMEM((width, width), BF16)],
        compiler_params=_params(("arbitrary",)),
        name="s5",
    )(proj, b_blockdiag(bb_re), b_blockdiag(bb_im), c_blockdiag(c_re), c_blockdiag(c_im),
      ab_re.reshape(n_vreg, SUBLANES, LANES), ab_im.reshape(n_vreg, SUBLANES, LANES),
      d_skip.reshape(1, width), w_glu)


def _ln_rows(h, g_ref, b_ref):
    mu = jnp.mean(h, axis=-1, keepdims=True)
    hc = h - mu
    var = jnp.mean(hc * hc, axis=-1, keepdims=True)
    return hc * lax.rsqrt(var + LN_EPS) * g_ref[...] + b_ref[...]


def _pack_bf16_halves(y):
    half = y.shape[1] // 2
    bits = lax.bitcast_convert_type(y.astype(BF16).astype(F32), jnp.uint32)
    return (bits[:, :half] >> 16) | (bits[:, half:] & jnp.uint32(0xFFFF0000))


def _unpack_bf16_halves(p):
    lo = lax.bitcast_convert_type(p << 16, F32)
    hi = lax.bitcast_convert_type(p & jnp.uint32(0xFFFF0000), F32)
    return jnp.concatenate([lo, hi], axis=1)


def _ln_router_kernel(x_ref, m_ref, g_ref, b_ref, wr_ref, o_ref, op_ref, lg_ref, *, alpha):
    y = _ln_rows(alpha * x_ref[...] + m_ref[...], g_ref, b_ref)
    o_ref[...] = y
    op_ref[...] = _pack_bf16_halves(y)
    lg_ref[...] = jnp.dot(y, wr_ref[...], preferred_element_type=F32,
                          precision=lax.Precision.HIGHEST)


def _ln_pair_kernel(x_ref, p0_ref, p1_ref, tw_ref, g_ref, b_ref, o_ref, ob_ref, *, alpha):
    tw = tw_ref[...]
    ffn = (tw[:, 0:1] * _unpack_bf16_halves(p0_ref[...])
           + tw[:, 1:2] * _unpack_bf16_halves(p1_ref[...]))
    y = _ln_rows(alpha * x_ref[...] + ffn, g_ref, b_ref)
    o_ref[...] = y
    ob_ref[...] = y.astype(BF16)


def _residual_layer_norm(x, addends, g, b, alpha, w_router=None):
    seq, dm = x.shape
    tm = _pick_tile(seq, (256, 128))
    row = pl.BlockSpec((tm, dm), lambda i: (i, 0))
    half_row = pl.BlockSpec((tm, dm // 2), lambda i: (i, 0))
    vec = pl.BlockSpec((1, dm), lambda i: (0, 0))
    args = [x, *addends, g.reshape(1, dm), b.reshape(1, dm)]
    if w_router is None:
        assert len(addends) == 3
        kernel = functools.partial(_ln_pair_kernel, alpha=alpha)
        in_specs = [row, half_row, half_row, pl.BlockSpec((tm, TOP_K), lambda i: (i, 0)), vec, vec]
        out_shape = [jax.ShapeDtypeStruct((seq, dm), F32), jax.ShapeDtypeStruct((seq, dm), BF16)]
        out_specs = [row, row]
    else:
        assert len(addends) == 1
        kernel = functools.partial(_ln_router_kernel, alpha=alpha)
        nr = w_router.shape[1]
        args.append(w_router)
        in_specs = [row, row, vec, vec, pl.BlockSpec((dm, nr), lambda i: (0, 0))]
        out_shape = [jax.ShapeDtypeStruct((seq, dm), F32),
                     jax.ShapeDtypeStruct((seq, dm // 2), jnp.uint32),
                     jax.ShapeDtypeStruct((seq, nr), F32)]
        out_specs = [row, half_row, pl.BlockSpec((tm, nr), lambda i: (i, 0))]
    return pl.pallas_call(
        kernel,
        out_shape=out_shape,
        grid=(seq // tm,),
        in_specs=in_specs,
        out_specs=out_specs,
        compiler_params=_params(("parallel",)),
        name="residual_layer_norm",
    )(*args)


def _moe_kernel(be_ref, first_ref, nxt_ref, nused_ref, x_ref, wg_hbm, wu_hbm, wd_hbm, o_ref,
                sg_ref, su_ref, sd_ref, bg_ref, bu_ref, bd_ref, sem, *, layer):
    b = pl.program_id(0)

    def weight_copies(e):
        return (pltpu.make_async_copy(wg_hbm.at[layer, e], sg_ref, sem.at[0]),
                pltpu.make_async_copy(wu_hbm.at[layer, e], su_ref, sem.at[1]),
                pltpu.make_async_copy(wd_hbm.at[layer, e], sd_ref, sem.at[2]))

    @pl.when(b == 0)
    def _():
        for cp in weight_copies(be_ref[0]):
            cp.start()

    @pl.when(first_ref[b] == 1)
    def _():
        for cp in weight_copies(be_ref[b]):
            cp.wait()
        _cast_rows_to_bf16(sg_ref, bg_ref, 512)
        _cast_rows_to_bf16(su_ref, bu_ref, 512)
        _cast_rows_to_bf16(sd_ref, bd_ref, 64)

        @pl.when(nxt_ref[b] >= 0)
        def _():
            for cp in weight_copies(nxt_ref[b]):
                cp.start()

    @pl.when(b >= nused_ref[0])
    def _():
        o_ref[...] = jnp.zeros_like(o_ref)

    @pl.when(b < nused_ref[0])
    def _():
        x = _unpack_bf16_halves(x_ref[...]).astype(BF16)
        g = jnp.dot(x, bg_ref[...], preferred_element_type=F32)
        u = jnp.dot(x, bu_ref[...], preferred_element_type=F32)
        hid = (g * _sigmoid(g)) * u
        y = jnp.dot(hid.astype(BF16), bd_ref[...], preferred_element_type=F32)
        o_ref[...] = _pack_bf16_halves(y)


def _moe_experts(xp, blk_expert, blk_first, blk_next, n_used, w_gate, w_up, w_down, layer, *, bm):
    n_rows = xp.shape[0]
    dm, de = w_gate.shape[2:]
    n_blocks = n_rows // bm
    assert dm % 512 == 0 and de % 64 == 0
    grid_spec = pltpu.PrefetchScalarGridSpec(
        num_scalar_prefetch=4,
        grid=(n_blocks,),
        in_specs=[pl.BlockSpec((bm, dm // 2), lambda i, be, fi, nx, nu: (i, 0)),
                  pl.BlockSpec(memory_space=pl.ANY),
                  pl.BlockSpec(memory_space=pl.ANY),
                  pl.BlockSpec(memory_space=pl.ANY)],
        out_specs=pl.BlockSpec((bm, dm // 2), lambda i, be, fi, nx, nu: (i, 0)),
        scratch_shapes=[pltpu.VMEM((dm, de), F32), pltpu.VMEM((dm, de), F32),
                        pltpu.VMEM((de, dm), F32),
                        pltpu.VMEM((dm, de), BF16), pltpu.VMEM((dm, de), BF16),
                        pltpu.VMEM((de, dm), BF16),
                        pltpu.SemaphoreType.DMA((3,))],
    )
    return pl.pallas_call(
        functools.partial(_moe_kernel, layer=layer),
        out_shape=jax.ShapeDtypeStruct((n_rows, dm // 2), jnp.uint32),
        grid_spec=grid_spec,
        compiler_params=_params(("arbitrary",)),
        name="moe_experts",
    )(blk_expert, blk_first, blk_next, n_used, xp, w_gate, w_up, w_down)


def _moe(xp, logits, b_rg, b_re, w_gate, w_up, w_down, layer, *, bm):
    n_tok = xp.shape[0]
    n_experts = N_GROUPS * EXPERTS_PER_GROUP
    grp_logits = logits[:, :N_GROUPS] + b_rg
    exp_logits = logits[:, N_GROUPS:N_GROUPS + n_experts].reshape(n_tok, N_GROUPS, EXPERTS_PER_GROUP) + b_re
    grp_probs = jax.nn.softmax(grp_logits, axis=-1)
    grp_idx = jnp.argmax(grp_probs, axis=-1).astype(jnp.int32)[:, None]
    grp_p = jnp.max(grp_probs, axis=-1, keepdims=True)
    sel = exp_logits[:, 0]
    for g in range(1, N_GROUPS):
        sel = jnp.where(grp_idx == g, exp_logits[:, g], sel)
    lane = jnp.arange(EXPERTS_PER_GROUP, dtype=jnp.int32)[None, :]
    idx1 = jnp.argmax(sel, axis=-1).astype(jnp.int32)[:, None]
    rest = jnp.where(lane == idx1, -jnp.inf, sel)
    idx2 = jnp.argmax(rest, axis=-1).astype(jnp.int32)[:, None]
    top_idx = jnp.concatenate([idx1, idx2], axis=1)
    top_logit = jnp.concatenate([jnp.max(sel, axis=-1, keepdims=True),
                                 jnp.max(rest, axis=-1, keepdims=True)], axis=1)
    top_w = jax.nn.softmax(top_logit, axis=-1) * grp_p
    expert_id = (grp_idx * EXPERTS_PER_GROUP + top_idx).reshape(-1).astype(jnp.int32)
    n_assign = n_tok * TOP_K
    onehot = (expert_id[:, None] == jnp.arange(n_experts, dtype=jnp.int32)[None, :]).astype(jnp.int32)
    csum = jnp.cumsum(onehot, axis=0)
    counts = csum[-1]
    padded = (counts + bm - 1) // bm * bm
    pend = jnp.cumsum(padded)
    pstart = pend - padded
    dest = jnp.sum(onehot * (pstart[None, :] + csum - 1), axis=1)
    n_blocks = (n_assign + bm - 1) // bm + n_experts
    n_rows = n_blocks * bm
    blk_start = jnp.arange(n_blocks, dtype=jnp.int32) * bm
    blk_expert = jnp.minimum(jnp.searchsorted(pend, blk_start, side='right'),
                             n_experts - 1).astype(jnp.int32)
    row_asg = jnp.full((n_rows,), -1, jnp.int32).at[dest].set(jnp.arange(n_assign, dtype=jnp.int32))
    row_tok = jnp.where(row_asg >= 0, row_asg // TOP_K,
                        jnp.arange(n_rows, dtype=jnp.int32) % n_tok)
    blk_first = jnp.concatenate([jnp.ones((1,), jnp.int32),
                                 (blk_expert[1:] != blk_expert[:-1]).astype(jnp.int32)])
    nxt_idx = jnp.searchsorted(blk_expert, blk_expert, side='right')
    blk_next = jnp.where(nxt_idx < n_blocks, blk_expert[jnp.minimum(nxt_idx, n_blocks - 1)],
                         -1).astype(jnp.int32)
    xg = xp[row_tok]
    n_used = (pend[-1:] // bm).astype(jnp.int32)
    yb = _moe_experts(xg, blk_expert, blk_first, blk_next, n_used, w_gate, w_up, w_down, layer, bm=bm)
    pos = dest.reshape(n_tok, TOP_K)
    return yb[pos[:, 0]], yb[pos[:, 1]], top_w


def kernel(x, w_in, gdn_conv_w, gdn_a_log, gdn_dt_bias, gdn_norm_w, s5_lambda_re, s5_lambda_im, s5_b_re, s5_b_im, s5_c_re, s5_c_im, s5_d, s5_log_dt, s5_w_glu, w_out, ln1_g, ln1_b, router_group_w, router_group_b, router_expert_w, router_expert_b, expert_w_gate, expert_w_up, expert_w_down, ln2_g, ln2_b):
    bsz, seq, dm = x.shape
    assert bsz == 1
    depth = w_in.shape[0]
    heads = gdn_a_log.shape[1]
    gdn_width = heads * GDN_HEAD_DIM
    s5_width = s5_d.shape[1]
    c_z = 4 * gdn_width
    c_b = c_z + 2 * heads
    alpha = (2 * depth) ** 0.25
    n_experts = N_GROUPS * EXPERTS_PER_GROUP
    router_cols = LANES
    moe_bm = 256
    tail_cols = s5_width + LANES

    w_in_t = jnp.swapaxes(w_in, 1, 2)
    xf = x.reshape(seq, dm)
    xb = xf.astype(BF16)
    for i in range(depth):
        w_rest = lax.slice(w_in_t, (i, c_z, 0), (i + 1, w_in_t.shape[1], dm))[0]
        w_tail = jnp.concatenate(
            [w_rest[2 * heads:], w_rest[:2 * heads], jnp.zeros((LANES - 2 * heads, dm), F32)], axis=0)
        proj = _matmul(xb, w_in_t, i, c_z, F32, transposed=True)
        proj_tail = _matmul(xb, w_tail[None], 0, tail_cols, F32, transposed=True)
        a_t = proj_tail[:, s5_width:s5_width + heads].T.reshape(heads, seq // GDN_CHUNK, GDN_CHUNK)
        b_t = proj_tail[:, s5_width + heads:s5_width + 2 * heads].T.reshape(
            heads, seq // GDN_CHUNK, GDN_CHUNK)
        y_gdn = _gdn(proj, gdn_conv_w[i], a_t, b_t, gdn_a_log[i], gdn_dt_bias[i], gdn_norm_w[i],
                     heads=heads)
        y_s5 = _s5(proj_tail, 0, s5_lambda_re[i], s5_lambda_im[i], s5_b_re[i], s5_b_im[i],
                   s5_c_re[i], s5_c_im[i], s5_d[i], s5_log_dt[i], s5_w_glu[i])
        mix = _matmul((y_gdn, y_s5), w_out, i, dm, F32)
        w_router = jnp.concatenate(
            [router_group_w[i],
             jnp.transpose(router_expert_w[i], (1, 0, 2)).reshape(dm, n_experts),
             jnp.zeros((dm, router_cols - N_GROUPS - n_experts), F32)], axis=1)
        xf, xp, logits = _residual_layer_norm(xf, (mix,), ln1_g[i], ln1_b[i], alpha, w_router)
        ffn_pair = _moe(xp, logits, router_group_b[i], router_expert_b[i],
                        expert_w_gate, expert_w_up, expert_w_down, i, bm=moe_bm)
        xf, xb = _residual_layer_norm(xf, ffn_pair, ln2_g[i], ln2_b[i], alpha)
    return xf.reshape(bsz, seq, dm)
```

```python
import functools
import math

import jax
import jax.numpy as jnp
from jax import lax
from jax.experimental import pallas as pl
from jax.experimental.pallas import tpu as pltpu

F32 = jnp.float32
BF16 = jnp.bfloat16

LANES = 128
SUBLANES = 8
VMEM_LIMIT_BYTES = 56 * 1024 * 1024

GDN_CHUNK = 64
GDN_HEAD_DIM = 128
SOLVE_CHUNKS = 4
CONV_WIDTH = 4
S5_GROUP = 16
S5_STATE = 64
S5_GROUPS_PER_SLAB = LANES // S5_GROUP
S5_SLAB_STATES = S5_GROUPS_PER_SLAB * S5_STATE
S5_TILES_PER_SLAB = S5_SLAB_STATES // LANES
N_GROUPS = 4
EXPERTS_PER_GROUP = 8
TOP_K = 2
LN_EPS = 1e-5
RMS_EPS = 1e-6
L2_EPS = 1e-6


def _params(sem):
    return pltpu.CompilerParams(dimension_semantics=sem, vmem_limit_bytes=VMEM_LIMIT_BYTES)


def _pick_tile(n, candidates):
    for c in candidates:
        if n % c == 0:
            return c
    return n


def _bdot(a, b):
    return jnp.dot(a.astype(BF16), b.astype(BF16), preferred_element_type=F32)


def _bdot_nt(a, b):
    return lax.dot_general(a.astype(BF16), b.astype(BF16), (((1,), (1,)), ((), ())),
                           preferred_element_type=F32)


def _bdot_tn(a, b):
    return lax.dot_general(a.astype(BF16), b.astype(BF16), (((0,), (0,)), ((), ())),
                           preferred_element_type=F32)


def _sigmoid(x):
    return 1.0 / (1.0 + jnp.exp(-x))


def _cast_rows_to_bf16(src_ref, dst_ref, rows_per_step):
    def body(r, carry):
        rows = pl.ds(pl.multiple_of(r * rows_per_step, rows_per_step), rows_per_step)
        dst_ref[rows, :] = src_ref[rows, :].astype(BF16)
        return carry

    lax.fori_loop(0, src_ref.shape[0] // rows_per_step, body, 0)


def _mm_kernel(*refs, n_pieces, cast_rows, transposed):
    a_refs, (b_ref, o_ref, bb_ref) = refs[:n_pieces], refs[n_pieces:]

    @pl.when(pl.program_id(1) == 0)
    def _():
        _cast_rows_to_bf16(b_ref, bb_ref, cast_rows)

    if transposed:
        (a_ref,) = a_refs
        acc = lax.dot_general(a_ref[...], bb_ref[...], (((1,), (1,)), ((), ())),
                              preferred_element_type=F32)
    else:
        acc, k0 = None, 0
        for a_ref in a_refs:
            k1 = k0 + a_ref.shape[1]
            part = jnp.dot(a_ref[...], bb_ref[k0:k1, :], preferred_element_type=F32)
            acc, k0 = part if acc is None else acc + part, k1
    o_ref[...] = acc.astype(o_ref.dtype)


def _matmul(a, w, layer, n_cols, out_dtype, *, transposed=False):
    pieces = a if isinstance(a, tuple) else (a,)
    m = pieces[0].shape[0]
    kdim = sum(p.shape[1] for p in pieces)
    tm = _pick_tile(m, (512, 256, 128))
    tn = _pick_tile(n_cols, (1024, 512, 384, 256, 128))
    if transposed:
        w_block, w_index, cast_rows = (None, tn, kdim), (lambda j, i: (layer, j, 0)), LANES
    else:
        w_block, w_index = (None, kdim, tn), (lambda j, i: (layer, 0, j))
        cast_rows = _pick_tile(kdim, (512, 256, 128))
    return pl.pallas_call(
        functools.partial(_mm_kernel, n_pieces=len(pieces), cast_rows=cast_rows,
                          transposed=transposed),
        out_shape=jax.ShapeDtypeStruct((m, n_cols), out_dtype),
        grid=(n_cols // tn, m // tm),
        in_specs=[pl.BlockSpec((tm, p.shape[1]), lambda j, i: (i, 0)) for p in pieces]
        + [pl.BlockSpec(w_block, w_index)],
        out_specs=pl.BlockSpec((tm, tn), lambda j, i: (i, j)),
        scratch_shapes=[pltpu.VMEM(w_block[1:], BF16)],
        compiler_params=_params(("arbitrary", "arbitrary")),
        name="matmul",
    )(*pieces, w)


def _gdn_kernel(q_ref, k_ref, v_ref, z_ref, wq_ref, wk_ref, wv_ref, a_ref, b_ref,
                alog_ref, dtb_ref, nw_ref, o_ref,
                s_ref, cq_ref, ck_ref, cv_ref, qs_ref, ks_ref, vs_ref, w_ref, qk_ref, dl_ref, ext_refs,
                *, tb, hb):
    c = GDN_CHUNK
    d = GDN_HEAD_DIM
    nc = tb // c

    @pl.when(pl.program_id(1) == 0)
    def _():
        s_ref[...] = jnp.zeros_like(s_ref)
        cq_ref[...] = jnp.zeros_like(cq_ref)
        ck_ref[...] = jnp.zeros_like(ck_ref)
        cv_ref[...] = jnp.zeros_like(cv_ref)

    def conv_body(ri, carry):
        r0 = pl.multiple_of(ri * c, c)
        rp = pl.multiple_of(jnp.maximum(r0 - SUBLANES, 0), SUBLANES)
        for a, (x_ref, c_ref, w_ref, dst_ref, scale) in enumerate(
                ((q_ref, cq_ref, wq_ref, qs_ref, d ** -0.5),
                 (k_ref, ck_ref, wk_ref, ks_ref, 1.0),
                 (v_ref, cv_ref, wv_ref, vs_ref, None))):
            for h in range(hb):
                sl = slice(h * d, (h + 1) * d)
                ext_ref = ext_refs.at[a * hb + h]
                ext_ref[0:SUBLANES, :] = jnp.where(ri == 0, c_ref[:, sl],
                                                   x_ref[pl.ds(rp, SUBLANES), sl])
                cur = x_ref[pl.ds(r0, c), sl]
                ext_ref[SUBLANES:, :] = cur
                w = w_ref[:, sl]
                acc = cur * w[CONV_WIDTH - 1:CONV_WIDTH, :]
                for j in range(1, CONV_WIDTH):
                    acc = acc + (ext_ref[SUBLANES - j:SUBLANES - j + c, :]
                                 * w[CONV_WIDTH - 1 - j:CONV_WIDTH - j, :])
                y = acc * _sigmoid(acc)
                if scale is not None:
                    y = y * (lax.rsqrt(jnp.sum(y * y, axis=-1, keepdims=True) + L2_EPS) * scale)
                dst_ref[pl.ds(r0, c), sl] = y
        return carry

    lax.fori_loop(0, nc, conv_body, 0)
    cq_ref[...] = q_ref[tb - SUBLANES:tb, :]
    ck_ref[...] = k_ref[tb - SUBLANES:tb, :]
    cv_ref[...] = v_ref[tb - SUBLANES:tb, :]

    ii = lax.broadcasted_iota(jnp.int32, (c, c), 0)
    jj = lax.broadcasted_iota(jnp.int32, (c, c), 1)
    causal = jj <= ii
    diag = jj == ii
    nw = nw_ref[...]

    heads = range(hb)
    sls = [slice(h * d, (h + 1) * d) for h in heads]

    ii2 = lax.broadcasted_iota(jnp.int32, (c, d), 0)
    ll2 = lax.broadcasted_iota(jnp.int32, (c, d), 1)
    left = ll2 < c
    jj2 = jnp.where(left, ll2, ll2 - c)
    causal2 = jj2 <= ii2
    strict2 = jj2 < ii2
    diag2 = jj2 == ii2
    eye2 = jnp.where(diag2, 1.0, 0.0).astype(F32)

    def block_diag(m):
        return jnp.concatenate([jnp.where(left, m, 0.0), jnp.where(left, 0.0, m)], axis=0)

    def side_by_side(a, b):
        z = jnp.zeros(a.shape, BF16)
        return jnp.concatenate([jnp.concatenate([a.astype(BF16), z], axis=1),
                                jnp.concatenate([z, b.astype(BF16)], axis=1)], axis=0)

    def solve_body(pi, carry):
        items = [(pi * SOLVE_CHUNKS + cj, h) for cj in range(SOLVE_CHUNKS) for h in heads]
        n = range(len(items))
        pairs = range(len(items) // 2)
        rows = [pl.ds(pl.multiple_of(ci * c, c), c) for ci, _ in items]
        qc = [qs_ref[rows[i], sls[h]] for i, (_, h) in enumerate(items)]
        kc = [ks_ref[rows[i], sls[h]] for i, (_, h) in enumerate(items)]
        vc = [vs_ref[rows[i], sls[h]] for i, (_, h) in enumerate(items)]
        decay_col, beta_col = [], []
        for ci, h in items:
            a_row = a_ref[h, pl.ds(ci, 1), :]
            b_row = b_ref[h, pl.ds(ci, 1), :]
            sp_in = a_row + dtb_ref[h]
            softplus = jnp.maximum(sp_in, 0.0) + jnp.log1p(jnp.exp(-jnp.abs(sp_in)))
            g_row = -jnp.exp(alog_ref[h]) * softplus
            beta_row = _sigmoid(b_row)
            g_b = jnp.broadcast_to(g_row, (c, c))
            decay_col.append(jnp.sum(jnp.where(causal, g_b, 0.0), axis=1, keepdims=True))
            beta_col.append(jnp.sum(jnp.where(diag, jnp.broadcast_to(beta_row, (c, c)), 0.0),
                                    axis=1, keepdims=True))
        gamma = []
        for j in pairs:
            dsel = jnp.where(left, decay_col[2 * j], decay_col[2 * j + 1])
            drow = jnp.sum(jnp.where(diag2, dsel, 0.0), axis=0, keepdims=True)
            gamma.append(jnp.where(causal2, jnp.exp(jnp.where(causal2, dsel - drow, 0.0)), 0.0))
        k_beta = [kc[i] * beta_col[i] for i in n]
        kq = [_bdot_nt(jnp.concatenate(
                  [jnp.concatenate([k_beta[2 * j].astype(BF16), k_beta[2 * j + 1].astype(BF16)], axis=1),
                   jnp.concatenate([qc[2 * j].astype(BF16), qc[2 * j + 1].astype(BF16)], axis=1)], axis=0),
                  side_by_side(kc[2 * j], kc[2 * j + 1])) for j in pairs]
        pw = [jnp.where(strict2, -(kq[j][:c, :] * gamma[j]), 0.0) for j in pairs]
        p = [eye2 + pw[j] for j in pairs]
        pw = [_bdot(pw[j], block_diag(pw[j])) for j in pairs]
        span = 2
        while 2 * span < c:
            st = [_bdot(jnp.concatenate([pw[j], p[j]], axis=0), block_diag(pw[j])) for j in pairs]
            p = [p[j] + st[j][c:, :] for j in pairs]
            pw = [st[j][:c, :] for j in pairs]
            span *= 2
        t_inv = [p[j] + _bdot(p[j], block_diag(pw[j])) for j in pairs]
        e_col = [jnp.exp(decay_col[i]) for i in n]
        rhs = [jnp.concatenate([(vc[i] * beta_col[i]).astype(BF16),
                                (k_beta[i] * e_col[i]).astype(BF16)], axis=1) for i in n]
        zero2 = jnp.zeros((c, 2 * d), BF16)
        uw = [_bdot(t_inv[j], jnp.concatenate(
                  [jnp.concatenate([rhs[2 * j], zero2], axis=1),
                   jnp.concatenate([zero2, rhs[2 * j + 1]], axis=1)], axis=0))
              for j in pairs]
        for i, (ci, h) in enumerate(items):
            j, off = i // 2, (i % 2) * 2 * d
            decay_last = decay_col[i][c - 1:c, :]
            vs_ref[rows[i], sls[h]] = uw[j][:, off:off + d]
            w_ref[rows[i], sls[h]] = uw[j][:, off + d:off + 2 * d]
            qs_ref[rows[i], sls[h]] = qc[i] * e_col[i]
            ks_ref[rows[i], sls[h]] = kc[i] * jnp.exp(decay_last - decay_col[i])
            dl_ref[ci * hb + h] = jnp.broadcast_to(jnp.exp(decay_last), (SUBLANES, d))
            if i % 2 == 0:
                qk_ref[h // 2, rows[i], :] = kq[j][c:, :] * gamma[j]
        return carry

    def state_body(ci, carry):
        rows = pl.ds(pl.multiple_of(ci * c, c), c)
        s = [s_ref[h] for h in heads]
        ws = [_bdot(jnp.concatenate([w_ref[rows, sls[h]], qs_ref[rows, sls[h]]], axis=0), s[h])
              for h in heads]
        v_new = [vs_ref[rows, sls[h]] - ws[h][:c, :] for h in heads]
        o_intra = [_bdot(qk_ref[j, rows, :], side_by_side(v_new[2 * j], v_new[2 * j + 1]))
                   for j in range(hb // 2)]
        kv = [_bdot_tn(ks_ref[rows, sls[h]], v_new[h]) for h in heads]
        for h in heads:
            s_ref[h] = s[h] * dl_ref[ci * hb + h][0:1, :] + kv[h]
            o = ws[h][c:, :] + o_intra[h // 2][:, (h % 2) * d:(h % 2 + 1) * d]
            o = o * lax.rsqrt(jnp.mean(o * o, axis=-1, keepdims=True) + RMS_EPS) * nw
            zc = z_ref[rows, sls[h]]
            o = o * (zc * _sigmoid(zc))
            o_ref[rows, sls[h]] = o.astype(o_ref.dtype)
        return carry

    lax.fori_loop(0, nc // SOLVE_CHUNKS, solve_body, 0)
    lax.fori_loop(0, nc, state_body, 0)


def _gdn(proj, conv_w, a_t, b_t, a_log, dt_bias, norm_w, *, heads):
    seq = proj.shape[0]
    d = GDN_HEAD_DIM
    hb = _pick_tile(heads, (12, 8, 6, 4, 2))
    assert hb % 2 == 0 and d == 2 * GDN_CHUNK
    tb = _pick_tile(seq, (512,))
    n_hb = heads // hb
    wblk = hb * d
    kernel = functools.partial(_gdn_kernel, tb=tb, hb=hb)

    def col(offset):
        return lambda h, t: (t, offset * n_hb + h)

    def wcol(offset):
        return lambda h, t: (0, offset * n_hb + h)

    return pl.pallas_call(
        kernel,
        out_shape=jax.ShapeDtypeStruct((seq, heads * d), BF16),
        grid=(n_hb, seq // tb),
        in_specs=[pl.BlockSpec((tb, wblk), col(0)),
                  pl.BlockSpec((tb, wblk), col(1)),
                  pl.BlockSpec((tb, wblk), col(2)),
                  pl.BlockSpec((tb, wblk), col(3)),
                  pl.BlockSpec((CONV_WIDTH, wblk), wcol(0)),
                  pl.BlockSpec((CONV_WIDTH, wblk), wcol(1)),
                  pl.BlockSpec((CONV_WIDTH, wblk), wcol(2)),
                  pl.BlockSpec((hb, tb // GDN_CHUNK, GDN_CHUNK), lambda h, t: (h, t, 0)),
                  pl.BlockSpec((hb, tb // GDN_CHUNK, GDN_CHUNK), lambda h, t: (h, t, 0)),
                  pl.BlockSpec((hb, 1, 1), lambda h, t: (h, 0, 0)),
                  pl.BlockSpec((hb, 1, 1), lambda h, t: (h, 0, 0)),
                  pl.BlockSpec((1, d), lambda h, t: (0, 0))],
        out_specs=pl.BlockSpec((tb, wblk), lambda h, t: (t, h)),
        scratch_shapes=[pltpu.VMEM((hb, d, d), F32),
                        pltpu.VMEM((SUBLANES, wblk), F32),
                        pltpu.VMEM((SUBLANES, wblk), F32),
                        pltpu.VMEM((SUBLANES, wblk), F32),
                        pltpu.VMEM((tb, wblk), F32),
                        pltpu.VMEM((tb, wblk), F32),
                        pltpu.VMEM((tb, wblk), F32),
                        pltpu.VMEM((tb, wblk), F32),
                        pltpu.VMEM((hb // 2, tb, d), F32),
                        pltpu.VMEM((tb // GDN_CHUNK * hb, SUBLANES, d), F32),
                        pltpu.VMEM((3 * hb, SUBLANES + GDN_CHUNK, d), F32)],
        compiler_params=_params(("parallel", "arbitrary")),
        name="gdn",
    )(proj, proj, proj, proj, conv_w, conv_w, conv_w, a_t, b_t,
      a_log.reshape(heads, 1, 1), dt_bias.reshape(heads, 1, 1), norm_w.reshape(1, d))


def _gelu_tanh(y):
    return 0.5 * y * (1.0 + jnp.tanh(math.sqrt(2.0 / math.pi) * (y + 0.044715 * (y * y * y))))


def _s5_kernel(u_ref, bre_ref, bim_ref, cre_ref, cim_ref, are_ref, aim_ref, d_ref, wglu_ref,
               o_ref, xr_ref, xi_ref, str_ref, sti_ref, y_ref, wglu_bf_ref, *, tb, pitch, slabs):
    tiles = slabs * S5_TILES_PER_SLAB
    n_vreg = tiles // SUBLANES

    @pl.when(pl.program_id(0) == 0)
    def _():
        str_ref[...] = jnp.zeros_like(str_ref)
        sti_ref[...] = jnp.zeros_like(sti_ref)
        _cast_rows_to_bf16(wglu_ref, wglu_bf_ref, LANES)

    for s in range(slabs):
        ub = u_ref[:, s * LANES:(s + 1) * LANES].astype(BF16)
        r = jnp.dot(ub, bre_ref[s], preferred_element_type=F32)
        m = jnp.dot(ub, bim_ref[s], preferred_element_type=F32)
        for t4 in range(S5_TILES_PER_SLAB):
            lt = s * S5_TILES_PER_SLAB + t4
            xr_ref[lt * pitch:lt * pitch + tb, :] = r[:, t4 * LANES:(t4 + 1) * LANES]
            xi_ref[lt * pitch:lt * pitch + tb, :] = m[:, t4 * LANES:(t4 + 1) * LANES]

    ar = [are_ref[j] for j in range(n_vreg)]
    ai = [aim_ref[j] for j in range(n_vreg)]

    def step(t, carry):
        xr, xi = carry
        nxr, nxi = [], []
        for j in range(n_vreg):
            rows = pl.ds(j * SUBLANES * pitch + t, SUBLANES, stride=pitch)
            br = xr_ref[rows, :]
            bi = xi_ref[rows, :]
            r = ar[j] * xr[j] - ai[j] * xi[j] + br
            m = ar[j] * xi[j] + ai[j] * xr[j] + bi
            xr_ref[rows, :] = r
            xi_ref[rows, :] = m
            nxr.append(r)
            nxi.append(m)
        return tuple(nxr), tuple(nxi)

    x0 = (tuple(str_ref[j] for j in range(n_vreg)), tuple(sti_ref[j] for j in range(n_vreg)))
    xr_f, xi_f = lax.fori_loop(0, tb, step, x0, unroll=8)
    for j in range(n_vreg):
        str_ref[j] = xr_f[j]
        sti_ref[j] = xi_f[j]

    for s in range(slabs):
        acc = jnp.zeros((tb, LANES), F32)
        for t4 in range(S5_TILES_PER_SLAB):
            lt = s * S5_TILES_PER_SLAB + t4
            xr = xr_ref[lt * pitch:lt * pitch + tb, :].astype(BF16)
            xi = xi_ref[lt * pitch:lt * pitch + tb, :].astype(BF16)
            acc = acc + jnp.dot(xr, cre_ref[s, t4 * LANES:(t4 + 1) * LANES, :],
                                preferred_element_type=F32)
            acc = acc - jnp.dot(xi, cim_ref[s, t4 * LANES:(t4 + 1) * LANES, :],
                                preferred_element_type=F32)
        sl = slice(s * LANES, (s + 1) * LANES)
        y_ref[:, sl] = _gelu_tanh(acc + d_ref[:, sl] * u_ref[:, sl])

    y = y_ref[...]
    gate = jnp.dot(y.astype(BF16), wglu_bf_ref[...], preferred_element_type=F32)
    o_ref[...] = (y * _sigmoid(gate)).astype(o_ref.dtype)


def _s5(proj, u_col_block, lam_re, lam_im, b_re, b_im, c_re, c_im, d_skip, log_dt, w_glu):
    seq = proj.shape[0]
    groups = lam_re.shape[0]
    width = groups * S5_GROUP
    slabs = width // LANES
    tiles = slabs * S5_TILES_PER_SLAB
    assert tiles % SUBLANES == 0
    tb = _pick_tile(seq, (512,))
    pitch = tb + SUBLANES

    dt = jnp.exp(log_dt)[:, None]
    mag = jnp.exp(lam_re * dt)
    ab_re, ab_im = mag * jnp.cos(lam_im * dt), mag * jnp.sin(lam_im * dt)
    den = lam_re * lam_re + lam_im * lam_im
    nr, ni = ab_re - 1.0, ab_im
    coef_re = (nr * lam_re + ni * lam_im) / den
    coef_im = (ni * lam_re - nr * lam_im) / den
    bb_re = coef_re[..., None] * b_re - coef_im[..., None] * b_im
    bb_im = coef_re[..., None] * b_im + coef_im[..., None] * b_re
    eye = jnp.eye(S5_GROUPS_PER_SLAB, dtype=F32)

    def b_blockdiag(bb):
        t = bb.reshape(slabs, S5_GROUPS_PER_SLAB, S5_STATE, S5_GROUP)
        m = jnp.einsum('saph,ab->sahbp', t, eye)
        return m.reshape(slabs, LANES, S5_SLAB_STATES).astype(BF16)

    def c_blockdiag(cc):
        t = cc.reshape(slabs, S5_GROUPS_PER_SLAB, S5_GROUP, S5_STATE)
        m = jnp.einsum('sahp,ab->sapbh', t, eye)
        return m.reshape(slabs, S5_SLAB_STATES, LANES).astype(BF16)

    n_vreg = tiles // SUBLANES
    kernel = functools.partial(_s5_kernel, tb=tb, pitch=pitch, slabs=slabs)
    full3 = lambda i: (0, 0, 0)
    return pl.pallas_call(
        kernel,
        out_shape=jax.ShapeDtypeStruct((seq, width), BF16),
        grid=(seq // tb,),
        in_specs=[pl.BlockSpec((tb, width), lambda i: (i, u_col_block)),
                  pl.BlockSpec((slabs, LANES, S5_SLAB_STATES), full3),
                  pl.BlockSpec((slabs, LANES, S5_SLAB_STATES), full3),
                  pl.BlockSpec((slabs, S5_SLAB_STATES, LANES), full3),
                  pl.BlockSpec((slabs, S5_SLAB_STATES, LANES), full3),
                  pl.BlockSpec((n_vreg, SUBLANES, LANES), full3),
                  pl.BlockSpec((n_vreg, SUBLANES, LANES), full3),
                  pl.BlockSpec((1, width), lambda i: (0, 0)),
                  pl.BlockSpec((width, width), lambda i: (0, 0))],
        out_specs=pl.BlockSpec((tb, width), lambda i: (i, 0)),
        scratch_shapes=[pltpu.VMEM((tiles * pitch, LANES), F32),
                        pltpu.VMEM((tiles * pitch, LANES), F32),
                        pltpu.VMEM((n_vreg, SUBLANES, LANES), F32),
                        pltpu.VMEM((n_vreg, SUBLANES, LANES), F32),
                        pltpu.VMEM((tb, width), F32),
                        pltpu.VMEM((width, width), BF16)],
        compiler_params=_params(("arbitrary",)),
        name="s5",
    )(proj, b_blockdiag(bb_re), b_blockdiag(bb_im), c_blockdiag(c_re), c_blockdiag(c_im),
      ab_re.reshape(n_vreg, SUBLANES, LANES), ab_im.reshape(n_vreg, SUBLANES, LANES),
      d_skip.reshape(1, width), w_glu)


def _ln_rows(h, g_ref, b_ref):
    mu = jnp.mean(h, axis=-1, keepdims=True)
    hc = h - mu
    var = jnp.mean(hc * hc, axis=-1, keepdims=True)
    return hc * lax.rsqrt(var + LN_EPS) * g_ref[...] + b_ref[...]


def _pack_bf16_halves(y):
    half = y.shape[1] // 2
    bits = lax.bitcast_convert_type(y.astype(BF16).astype(F32), jnp.uint32)
    return (bits[:, :half] >> 16) | (bits[:, half:] & jnp.uint32(0xFFFF0000))


def _unpack_bf16_halves(p):
    lo = lax.bitcast_convert_type(p << 16, F32)
    hi = lax.bitcast_convert_type(p & jnp.uint32(0xFFFF0000), F32)
    return jnp.concatenate([lo, hi], axis=1)


def _ln_router_kernel(x_ref, m_ref, g_ref, b_ref, wr_ref, o_ref, op_ref, lg_ref, *, alpha):
    y = _ln_rows(alpha * x_ref[...] + m_ref[...], g_ref, b_ref)
    o_ref[...] = y
    op_ref[...] = _pack_bf16_halves(y)
    lg_ref[...] = jnp.dot(y, wr_ref[...], preferred_element_type=F32,
                          precision=lax.Precision.HIGHEST)


def _ln_pair_kernel(x_ref, p0_ref, p1_ref, tw_ref, g_ref, b_ref, o_ref, ob_ref, *, alpha):
    tw = tw_ref[...]
    ffn = (tw[:, 0:1] * _unpack_bf16_halves(p0_ref[...])
           + tw[:, 1:2] * _unpack_bf16_halves(p1_ref[...]))
    y = _ln_rows(alpha * x_ref[...] + ffn, g_ref, b_ref)
    o_ref[...] = y
    ob_ref[...] = y.astype(BF16)


def _residual_layer_norm(x, addends, g, b, alpha, w_router=None):
    seq, dm = x.shape
    tm = _pick_tile(seq, (256, 128))
    row = pl.BlockSpec((tm, dm), lambda i: (i, 0))
    half_row = pl.BlockSpec((tm, dm // 2), lambda i: (i, 0))
    vec = pl.BlockSpec((1, dm), lambda i: (0, 0))
    args = [x, *addends, g.reshape(1, dm), b.reshape(1, dm)]
    if w_router is None:
        assert len(addends) == 3
        kernel = functools.partial(_ln_pair_kernel, alpha=alpha)
        in_specs = [row, half_row, half_row, pl.BlockSpec((tm, TOP_K), lambda i: (i, 0)), vec, vec]
        out_shape = [jax.ShapeDtypeStruct((seq, dm), F32), jax.ShapeDtypeStruct((seq, dm), BF16)]
        out_specs = [row, row]
    else:
        assert len(addends) == 1
        kernel = functools.partial(_ln_router_kernel, alpha=alpha)
        nr = w_router.shape[1]
        args.append(w_router)
        in_specs = [row, row, vec, vec, pl.BlockSpec((dm, nr), lambda i: (0, 0))]
        out_shape = [jax.ShapeDtypeStruct((seq, dm), F32),
                     jax.ShapeDtypeStruct((seq, dm // 2), jnp.uint32),
                     jax.ShapeDtypeStruct((seq, nr), F32)]
        out_specs = [row, half_row, pl.BlockSpec((tm, nr), lambda i: (i, 0))]
    return pl.pallas_call(
        kernel,
        out_shape=out_shape,
        grid=(seq // tm,),
        in_specs=in_specs,
        out_specs=out_specs,
        compiler_params=_params(("parallel",)),
        name="residual_layer_norm",
    )(*args)


def _moe_kernel(be_ref, first_ref, nxt_ref, nused_ref, x_ref, wg_hbm, wu_hbm, wd_hbm, o_ref,
                sg_ref, su_ref, sd_ref, bg_ref, bu_ref, bd_ref, sem, *, layer):
    b = pl.program_id(0)

    def weight_copies(e):
        return (pltpu.make_async_copy(wg_hbm.at[layer, e], sg_ref, sem.at[0]),
                pltpu.make_async_copy(wu_hbm.at[layer, e], su_ref, sem.at[1]),
                pltpu.make_async_copy(wd_hbm.at[layer, e], sd_ref, sem.at[2]))

    @pl.when(b == 0)
    def _():
        for cp in weight_copies(be_ref[0]):
            cp.start()

    @pl.when(first_ref[b] == 1)
    def _():
        for cp in weight_copies(be_ref[b]):
            cp.wait()
        _cast_rows_to_bf16(sg_ref, bg_ref, 512)
        _cast_rows_to_bf16(su_ref, bu_ref, 512)
        _cast_rows_to_bf16(sd_ref, bd_ref, 64)

        @pl.when(nxt_ref[b] >= 0)
        def _():
            for cp in weight_copies(nxt_ref[b]):
                cp.start()

    @pl.when(b >= nused_ref[0])
    def _():
        o_ref[...] = jnp.zeros_like(o_ref)

    @pl.when(b < nused_ref[0])
    def _():
        x = _unpack_bf16_halves(x_ref[...]).astype(BF16)
        g = jnp.dot(x, bg_ref[...], preferred_element_type=F32)
        u = jnp.dot(x, bu_ref[...], preferred_element_type=F32)
        hid = (g * _sigmoid(g)) * u
        y = jnp.dot(hid.astype(BF16), bd_ref[...], preferred_element_type=F32)
        o_ref[...] = _pack_bf16_halves(y)


def _moe_experts(xp, blk_expert, blk_first, blk_next, n_used, w_gate, w_up, w_down, layer, *, bm):
    n_rows = xp.shape[0]
    dm, de = w_gate.shape[2:]
    n_blocks = n_rows // bm
    assert dm % 512 == 0 and de % 64 == 0
    grid_spec = pltpu.PrefetchScalarGridSpec(
        num_scalar_prefetch=4,
        grid=(n_blocks,),
        in_specs=[pl.BlockSpec((bm, dm // 2), lambda i, be, fi, nx, nu: (i, 0)),
                  pl.BlockSpec(memory_space=pl.ANY),
                  pl.BlockSpec(memory_space=pl.ANY),
                  pl.BlockSpec(memory_space=pl.ANY)],
        out_specs=pl.BlockSpec((bm, dm // 2), lambda i, be, fi, nx, nu: (i, 0)),
        scratch_shapes=[pltpu.VMEM((dm, de), F32), pltpu.VMEM((dm, de), F32),
                        pltpu.VMEM((de, dm), F32),
                        pltpu.VMEM((dm, de), BF16), pltpu.VMEM((dm, de), BF16),
                        pltpu.VMEM((de, dm), BF16),
                        pltpu.SemaphoreType.DMA((3,))],
    )
    return pl.pallas_call(
        functools.partial(_moe_kernel, layer=layer),
        out_shape=jax.ShapeDtypeStruct((n_rows, dm // 2), jnp.uint32),
        grid_spec=grid_spec,
        compiler_params=_params(("arbitrary",)),
        name="moe_experts",
    )(blk_expert, blk_first, blk_next, n_used, xp, w_gate, w_up, w_down)


def _moe(xp, logits, b_rg, b_re, w_gate, w_up, w_down, layer, *, bm):
    n_tok = xp.shape[0]
    n_experts = N_GROUPS * EXPERTS_PER_GROUP
    grp_logits = logits[:, :N_GROUPS] + b_rg
    exp_logits = logits[:, N_GROUPS:N_GROUPS + n_experts].reshape(n_tok, N_GROUPS, EXPERTS_PER_GROUP) + b_re
    grp_probs = jax.nn.softmax(grp_logits, axis=-1)
    grp_idx = jnp.argmax(grp_probs, axis=-1).astype(jnp.int32)[:, None]
    grp_p = jnp.max(grp_probs, axis=-1, keepdims=True)
    sel = exp_logits[:, 0]
    for g in range(1, N_GROUPS):
        sel = jnp.where(grp_idx == g, exp_logits[:, g], sel)
    lane = jnp.arange(EXPERTS_PER_GROUP, dtype=jnp.int32)[None, :]
    idx1 = jnp.argmax(sel, axis=-1).astype(jnp.int32)[:, None]
    rest = jnp.where(lane == idx1, -jnp.inf, sel)
    idx2 = jnp.argmax(rest, axis=-1).astype(jnp.int32)[:, None]
    top_idx = jnp.concatenate([idx1, idx2], axis=1)
    top_logit = jnp.concatenate([jnp.max(sel, axis=-1, keepdims=True),
                                 jnp.max(rest, axis=-1, keepdims=True)], axis=1)
    top_w = jax.nn.softmax(top_logit, axis=-1) * grp_p
    expert_id = (grp_idx * EXPERTS_PER_GROUP + top_idx).reshape(-1).astype(jnp.int32)
    n_assign = n_tok * TOP_K
    onehot = (expert_id[:, None] == jnp.arange(n_experts, dtype=jnp.int32)[None, :]).astype(jnp.int32)
    csum = jnp.cumsum(onehot, axis=0)
    counts = csum[-1]
    padded = (counts + bm - 1) // bm * bm
    pend = jnp.cumsum(padded)
    pstart = pend - padded
    dest = jnp.sum(onehot * (pstart[None, :] + csum - 1), axis=1)
    n_blocks = (n_assign + bm - 1) // bm + n_experts
    n_rows = n_blocks * bm
    blk_start = jnp.arange(n_blocks, dtype=jnp.int32) * bm
    blk_expert = jnp.minimum(jnp.searchsorted(pend, blk_start, side='right'),
                             n_experts - 1).astype(jnp.int32)
    row_asg = jnp.full((n_rows,), -1, jnp.int32).at[dest].set(jnp.arange(n_assign, dtype=jnp.int32))
    row_tok = jnp.where(row_asg >= 0, row_asg // TOP_K,
                        jnp.arange(n_rows, dtype=jnp.int32) % n_tok)
    blk_first = jnp.concatenate([jnp.ones((1,), jnp.int32),
                                 (blk_expert[1:] != blk_expert[:-1]).astype(jnp.int32)])
    nxt_idx = jnp.searchsorted(blk_expert, blk_expert, side='right')
    blk_next = jnp.where(nxt_idx < n_blocks, blk_expert[jnp.minimum(nxt_idx, n_blocks - 1)],
                         -1).astype(jnp.int32)
    xg = xp[row_tok]
    n_used = (pend[-1:] // bm).astype(jnp.int32)
    yb = _moe_experts(xg, blk_expert, blk_first, blk_next, n_used, w_gate, w_up, w_down, layer, bm=bm)
    pos = dest.reshape(n_tok, TOP_K)
    return yb[pos[:, 0]], yb[pos[:, 1]], top_w


def kernel(x, w_in, gdn_conv_w, gdn_a_log, gdn_dt_bias, gdn_norm_w, s5_lambda_re, s5_lambda_im, s5_b_re, s5_b_im, s5_c_re, s5_c_im, s5_d, s5_log_dt, s5_w_glu, w_out, ln1_g, ln1_b, router_group_w, router_group_b, router_expert_w, router_expert_b, expert_w_gate, expert_w_up, expert_w_down, ln2_g, ln2_b):
    bsz, seq, dm = x.shape
    assert bsz == 1
    depth = w_in.shape[0]
    heads = gdn_a_log.shape[1]
    gdn_width = heads * GDN_HEAD_DIM
    s5_width = s5_d.shape[1]
    c_z = 4 * gdn_width
    c_b = c_z + 2 * heads
    alpha = (2 * depth) ** 0.25
    n_experts = N_GROUPS * EXPERTS_PER_GROUP
    router_cols = LANES
    moe_bm = 256
    tail_cols = s5_width + LANES

    w_in_t = jnp.swapaxes(w_in, 1, 2)
    xf = x.reshape(seq, dm)
    xb = xf.astype(BF16)
    for i in range(depth):
        w_rest = lax.slice(w_in_t, (i, c_z, 0), (i + 1, w_in_t.shape[1], dm))[0]
        w_tail = jnp.concatenate(
            [w_rest[2 * heads:], w_rest[:2 * heads], jnp.zeros((LANES - 2 * heads, dm), F32)], axis=0)
        proj = _matmul(xb, w_in_t, i, c_z, F32, transposed=True)
        proj_tail = _matmul(xb, w_tail[None], 0, tail_cols, F32, transposed=True)
        a_t = proj_tail[:, s5_width:s5_width + heads].T.reshape(heads, seq // GDN_CHUNK, GDN_CHUNK)
        b_t = proj_tail[:, s5_width + heads:s5_width + 2 * heads].T.reshape(
            heads, seq // GDN_CHUNK, GDN_CHUNK)
        y_gdn = _gdn(proj, gdn_conv_w[i], a_t, b_t, gdn_a_log[i], gdn_dt_bias[i], gdn_norm_w[i],
                     heads=heads)
        y_s5 = _s5(proj_tail, 0, s5_lambda_re[i], s5_lambda_im[i], s5_b_re[i], s5_b_im[i],
                   s5_c_re[i], s5_c_im[i], s5_d[i], s5_log_dt[i], s5_w_glu[i])
        mix = _matmul((y_gdn, y_s5), w_out, i, dm, F32)
        w_router = jnp.concatenate(
            [router_group_w[i],
             jnp.transpose(router_expert_w[i], (1, 0, 2)).reshape(dm, n_experts),
             jnp.zeros((dm, router_cols - N_GROUPS - n_experts), F32)], axis=1)
        xf, xp, logits = _residual_layer_norm(xf, (mix,), ln1_g[i], ln1_b[i], alpha, w_router)
        ffn_pair = _moe(xp, logits, router_group_b[i], router_expert_b[i],
                        expert_w_gate, expert_w_up, expert_w_down, i, bm=moe_bm)
        xf, xb = _residual_layer_norm(xf, ffn_pair, ln2_g[i], ln2_b[i], alpha)
    return xf.reshape(bsz, seq, dm)
```
